```python
import math
import jax, jax.numpy as jnp
from jax import lax
import numpy as np


D_MODEL = 1024
BATCH = 2
SEQ = 8192
DEPTH = 4

CONV_WIDTH = D_MODEL // 2
CONV_K = 3
N_HEADS = 4
HEAD_DIM = D_MODEL // 16
V_DIM = 2 * HEAD_DIM
ATTN_WIDTH = N_HEADS * V_DIM
N_BRANCH = 2
D_FF = 4 * D_MODEL
Q_BLOCK = 128
EPS = 1e-6
IN_WIDTH = 3 * CONV_WIDTH + 3 * ATTN_WIDTH + N_BRANCH * D_MODEL

kernel_name = "hybrid_shortconv_diffattn_alibi_encoder"


def rmsnorm(x, g):
    xf = x.astype(jnp.float32)
    y = xf * lax.rsqrt(jnp.mean(xf * xf, axis=-1, keepdims=True) + EPS)
    return (y * g.astype(jnp.float32)).astype(x.dtype)


def alibi_slopes():
    i = jnp.arange(1, N_HEADS + 1, dtype=jnp.float32)
    return jnp.exp2(-8.0 * i / N_HEADS)


def diff_attention(q1, q2, k1, k2, v, lam):
    b, h, s, _ = q1.shape
    nb = s // Q_BLOCK
    scale = HEAD_DIM ** -0.5
    slopes = alibi_slopes()
    kpos = jnp.arange(s, dtype=jnp.float32)

    def to_blocks(q):
        return q.reshape(b, h, nb, Q_BLOCK, HEAD_DIM).transpose(2, 0, 1, 3, 4)

    starts = jnp.arange(nb, dtype=jnp.int32) * Q_BLOCK

    def block(args):
        q1b, q2b, start = args
        qpos = (start + jnp.arange(Q_BLOCK, dtype=jnp.int32)).astype(jnp.float32)
        bias = -slopes[:, None, None] * jnp.abs(qpos[:, None] - kpos[None, :])
        s1 = jnp.einsum('bhqd,bhkd->bhqk', q1b, k1).astype(jnp.float32) * scale + bias
        s2 = jnp.einsum('bhqd,bhkd->bhqk', q2b, k2).astype(jnp.float32) * scale + bias
        a = jax.nn.softmax(s1, axis=-1) - lam * jax.nn.softmax(s2, axis=-1)
        return jnp.einsum('bhqk,bhkv->bhqv', a.astype(v.dtype), v)

    o = lax.map(block, (to_blocks(q1), to_blocks(q2), starts))
    return o.transpose(1, 2, 0, 3, 4).reshape(b, h, s, V_DIM)


def mixing_block(u, w_in, b_gate, conv_w, w_conv_out, lambda_qk, subln_g,
                 w_attn_out, w_o, lam_init):
    b, s, _ = u.shape
    proj = u @ w_in
    cw, aw = CONV_WIDTH, ATTN_WIDTH
    offs = [cw, 2 * cw, 3 * cw, 3 * cw + aw, 3 * cw + 2 * aw, 3 * cw + 3 * aw]
    hc, bc, cc, q, k, v, gates = jnp.split(proj, offs, axis=-1)

    inner = cc * hc
    pad = jnp.pad(inner, ((0, 0), (CONV_K // 2, CONV_K // 2), (0, 0)))
    conv = pad[:, 0:s] * conv_w[0]
    for tap in range(1, CONV_K):
        conv = conv + pad[:, tap:tap + s] * conv_w[tap]
    y_conv = (bc * conv) @ w_conv_out

    q = q.reshape(b, s, N_HEADS, 2, HEAD_DIM)
    k = k.reshape(b, s, N_HEADS, 2, HEAD_DIM)
    q1 = q[:, :, :, 0].transpose(0, 2, 1, 3)
    q2 = q[:, :, :, 1].transpose(0, 2, 1, 3)
    k1 = k[:, :, :, 0].transpose(0, 2, 1, 3)
    k2 = k[:, :, :, 1].transpose(0, 2, 1, 3)
    vh = v.reshape(b, s, N_HEADS, V_DIM).transpose(0, 2, 1, 3)
    lq = lambda_qk.astype(jnp.float32)
    lam = (jnp.exp(jnp.sum(lq[0] * lq[1])) - jnp.exp(jnp.sum(lq[2] * lq[3]))
           + lam_init)
    o = diff_attention(q1, q2, k1, k2, vh, lam)
    o = rmsnorm(o, subln_g) * (1.0 - lam_init)
    o = o.transpose(0, 2, 1, 3).reshape(b, s, ATTN_WIDTH)
    y_attn = o @ w_attn_out

    g = jax.nn.sigmoid((gates + b_gate.reshape(-1)).astype(jnp.float32))
    g = g.reshape(b, s, N_BRANCH, D_MODEL)
    merged = (g[:, :, 0] * y_conv + g[:, :, 1] * y_attn).astype(u.dtype)
    return merged @ w_o


def setup_inputs(seed: int = 0) -> dict:
    key = jax.random.key(seed)
    ks = jax.random.split(key, 12)
    f32 = jnp.float32
    n = lambda k, shp: jax.random.normal(k, shp, dtype=f32)
    return {
        "x": n(ks[0], (BATCH, SEQ, D_MODEL)),
        "w_in": n(ks[1], (DEPTH, D_MODEL, IN_WIDTH)) * D_MODEL ** -0.5,
        "b_gate": 0.01 * n(ks[2], (DEPTH, N_BRANCH, D_MODEL)),
        "conv_w": n(ks[3], (DEPTH, CONV_K, CONV_WIDTH)) * CONV_K ** -0.5,
        "w_conv_out": n(ks[4], (DEPTH, CONV_WIDTH, D_MODEL)) * CONV_WIDTH ** -0.5,
        "lambda_qk": 0.1 * n(ks[5], (DEPTH, 4, HEAD_DIM)),
        "subln_g": 1.0 + 0.02 * n(ks[6], (DEPTH, V_DIM)),
        "w_attn_out": n(ks[7], (DEPTH, ATTN_WIDTH, D_MODEL)) * ATTN_WIDTH ** -0.5,
        "w_o": n(ks[8], (DEPTH, D_MODEL, D_MODEL)) * D_MODEL ** -0.5,
        "norm_g": 1.0 + 0.02 * n(ks[9], (DEPTH, 4, D_MODEL)),
        "w_ff1": n(ks[10], (DEPTH, D_MODEL, D_FF)) * D_MODEL ** -0.5,
        "w_ff2": n(ks[11], (DEPTH, D_FF, D_MODEL)) * D_FF ** -0.5,
    }


def reference(x, w_in, b_gate, conv_w, w_conv_out, lambda_qk, subln_g,
              w_attn_out, w_o, norm_g, w_ff1, w_ff2):
    h = x
    for l in range(DEPTH):
        lam_init = 0.8 - 0.6 * math.exp(-0.3 * l)
        u = rmsnorm(h, norm_g[l, 0])
        m = mixing_block(u, w_in[l], b_gate[l], conv_w[l], w_conv_out[l],
                         lambda_qk[l], subln_g[l], w_attn_out[l], w_o[l], lam_init)
        h = h + rmsnorm(m, norm_g[l, 1])
        u = rmsnorm(h, norm_g[l, 2])
        f = jnp.square(jax.nn.relu(u @ w_ff1[l])) @ w_ff2[l]
        h = h + rmsnorm(f, norm_g[l, 3])
    return h
```

```python
import functools
import math

import jax
import jax.numpy as jnp
from jax import lax
from jax.experimental import pallas as pl
from jax.experimental.pallas import tpu as pltpu

D_MODEL = 1024
CONV_WIDTH = 512
CONV_K = 3
N_HEADS = 4
HEAD_DIM = 64
V_DIM = 128
ATTN_WIDTH = N_HEADS * V_DIM
D_FF = 4 * D_MODEL
EPS = 1e-6

F32 = jnp.float32
BF16 = jnp.bfloat16

V7X_LANES = 128
V7X_SUBLANES = 8
V7X_BF16_ROWS_PER_VREG = 16
V7X_MXU_DIM = 256
V7X_VMEM_BYTES = 64 * 1024 * 1024

ROW_TILE = 512
Q_TILE = V7X_MXU_DIM
K_TILE = ROW_TILE
K_LANES = 2 * HEAD_DIM
N_AUX = 3
V_ROWS = V_DIM + V7X_BF16_ROWS_PER_VREG
LOG2E = math.log2(math.e)
Q_SCALE = HEAD_DIM ** -0.5 * LOG2E
M_INIT = -1e30

W_NN_COLS = 3 * CONV_WIDTH + 2 * N_HEADS * K_LANES + 2 * D_MODEL
OFF_BC, OFF_CC, OFF_K, OFF_G = 512, 1024, 1536, 1536 + 2 * N_HEADS * K_LANES


def _vmem_limit(block_bytes, temp_bytes):
    need = 2 * block_bytes + temp_bytes
    assert need < V7X_VMEM_BYTES, need
    return int(need)


def _nbytes(shape, dtype):
    return math.prod(shape) * jnp.dtype(dtype).itemsize


def _rms(x, g):
    return x * lax.rsqrt(jnp.mean(x * x, axis=-1, keepdims=True) + EPS) * g


def _inproj_kernel(h_ref, g_ref, wnn_ref, wt_ref, kaux_ref,
                   inner_ref, bc_ref, gates_ref, kp_ref, qt_ref, vt_ref):
    u = _rms(h_ref[...], g_ref[...]).astype(BF16)

    def nn(lo, hi):
        return jnp.dot(u, wnn_ref[:, lo:hi], preferred_element_type=F32)

    inner_ref[...] = nn(OFF_CC, OFF_K) * nn(0, OFF_BC)
    bc_ref[...] = nn(OFF_BC, OFF_CC)
    kp_ref[...] = (nn(OFF_K, OFF_G) + kaux_ref[...]).astype(BF16)
    gates_ref[:, :D_MODEL] = nn(OFF_G, OFF_G + D_MODEL)
    gates_ref[:, D_MODEL:] = nn(OFF_G + D_MODEL, W_NN_COLS)

    qv = lax.dot_general(wt_ref[...], u, (((1,), (1,)), ((), ())),
                         preferred_element_type=F32)
    qt_ref[0] = (qv[:ATTN_WIDTH] * Q_SCALE).astype(BF16)
    tm = u.shape[0]
    ones_rows = jnp.where(
        lax.broadcasted_iota(jnp.int32, (V_ROWS - V_DIM, tm), 0) == 0, 1.0, 0.0).astype(BF16)
    for h in range(N_HEADS):
        lo = ATTN_WIDTH + h * V_DIM
        vt_ref[0, h, 0, :V_DIM, :] = qv[lo:lo + V_DIM].astype(BF16)
        vt_ref[0, h, 0, V_DIM:, :] = ones_rows


def _inproj(h, g, wnn, wt, kaux, batch, seq):
    n_tok = h.shape[0]
    tm = ROW_TILE
    tpb = seq // tm
    grid = (n_tok // tm,)
    row = lambda i: (i, 0)
    const = lambda i: (0, 0)
    out_shape = (
        jax.ShapeDtypeStruct((n_tok, CONV_WIDTH), F32),
        jax.ShapeDtypeStruct((n_tok, CONV_WIDTH), F32),
        jax.ShapeDtypeStruct((n_tok, 2 * D_MODEL), F32),
        jax.ShapeDtypeStruct((n_tok, 2 * N_HEADS * K_LANES), BF16),
        jax.ShapeDtypeStruct((batch, ATTN_WIDTH, seq), BF16),
        jax.ShapeDtypeStruct((batch, N_HEADS, tpb, V_ROWS, tm), BF16),
    )
    in_specs = [
        pl.BlockSpec((tm, D_MODEL), row),
        pl.BlockSpec((1, D_MODEL), const),
        pl.BlockSpec((D_MODEL, W_NN_COLS), const),
        pl.BlockSpec((2 * ATTN_WIDTH, D_MODEL), const),
        pl.BlockSpec((tm, 2 * N_HEADS * K_LANES), const),
    ]
    out_specs = (
        pl.BlockSpec((tm, CONV_WIDTH), row),
        pl.BlockSpec((tm, CONV_WIDTH), row),
        pl.BlockSpec((tm, 2 * D_MODEL), row),
        pl.BlockSpec((tm, 2 * N_HEADS * K_LANES), row),
        pl.BlockSpec((1, ATTN_WIDTH, tm), lambda i: (i // tpb, 0, i % tpb)),
        pl.BlockSpec((1, N_HEADS, 1, V_ROWS, tm), lambda i: (i // tpb, 0, i % tpb, 0, 0)),
    )
    blocks = (_nbytes((tm, D_MODEL), F32) + _nbytes((D_MODEL, W_NN_COLS), BF16)
              + _nbytes((2 * ATTN_WIDTH, D_MODEL), BF16) + _nbytes((tm, 1024), F32)
              + 2 * _nbytes((tm, CONV_WIDTH), F32) + _nbytes((tm, 2 * D_MODEL), F32)
              + _nbytes((tm, 1024), BF16) + _nbytes((ATTN_WIDTH, tm), BF16)
              + _nbytes((N_HEADS, V_ROWS, tm), BF16))
    temps = 4 * _nbytes((tm, D_MODEL), F32) + _nbytes((2 * ATTN_WIDTH, tm), F32)
    return pl.pallas_call(
        _inproj_kernel, grid=grid, in_specs=in_specs, out_specs=out_specs, out_shape=out_shape,
        compiler_params=pltpu.CompilerParams(
            dimension_semantics=("arbitrary",), vmem_limit_bytes=_vmem_limit(blocks, temps)),
        name="inproj",
    )(h, g, wnn, wt, kaux)


def _attn_kernel(cs_ref, lq_ref, g_ref, qt_ref, kp_ref, vt_ref, o_ref, acc_ref, *, lam_init):
    h = pl.program_id(1)
    qi = pl.program_id(2)
    tq, tk = Q_TILE, K_TILE
    n_kb = kp_ref.shape[1] // tk
    c = cs_ref[h]
    q0 = qi * tq
    kd = q0 // tk

    aux_rows = lax.broadcasted_iota(jnp.int32, (K_LANES - HEAD_DIM, tq), 0) < N_AUX
    aux = {s: jnp.where(aux_rows, s, 0.0).astype(BF16) for s in (1.0, -1.0, 0.0)}

    def q_aug(m, sign):
        return jnp.concatenate([qt_ref[0, m * HEAD_DIM:(m + 1) * HEAD_DIM, :], aux[sign]], axis=0)

    ci = c * (q0 + lax.broadcasted_iota(jnp.int32, (1, tq), 1)).astype(F32)

    acc_ref[...] = jnp.zeros_like(acc_ref)

    def tile(kb, ms, sign, bias):
        k0 = (kb * tk).astype(F32)
        off = sign * (c * k0 - ci) if sign else None
        kblk = kp_ref[0, pl.ds(pl.multiple_of(kb * tk, tk), tk), :]
        vblk = vt_ref[0, 0, kb]
        new_ms = []
        for m in range(2):
            st = jnp.dot(kblk[:, m * K_LANES:(m + 1) * K_LANES], q_aug(m, float(sign)),
                         preferred_element_type=F32)
            if bias is not None:
                st = st + bias
            cm = jnp.max(st, axis=0, keepdims=True)
            if off is not None:
                cm = cm + off
            m_new = jnp.maximum(ms[m], cm)
            alpha = jnp.exp2(ms[m] - m_new)
            sub = m_new - off if off is not None else m_new
            p = jnp.exp2(st - sub).astype(BF16)
            acc_ref[m] = acc_ref[m] * alpha + jnp.dot(vblk, p, preferred_element_type=F32)
            new_ms.append(m_new)
        return tuple(new_ms)

    ii = lax.broadcasted_iota(jnp.int32, (tk, tq), 1) + (q0 - kd * tk)
    jj = lax.broadcasted_iota(jnp.int32, (tk, tq), 0)
    bias_d = jnp.abs(ii - jj).astype(F32) * (-c)
    m0 = jnp.full((1, tq), M_INIT, F32)
    ms = tile(kd, (m0, m0), 0, bias_d)
    ms = lax.fori_loop(0, kd, lambda kb, ms: tile(kb, ms, 1, None), ms)
    ms = lax.fori_loop(kd + 1, n_kb, lambda kb, ms: tile(kb, ms, -1, None), ms)

    lq = lq_ref[...]
    lam = (jnp.exp(jnp.sum(lq[0:1] * lq[1:2], axis=-1, keepdims=True))
           - jnp.exp(jnp.sum(lq[2:3] * lq[3:4], axis=-1, keepdims=True)) + lam_init)
    o1 = acc_ref[0, :V_DIM, :] / acc_ref[0, V_DIM:V_DIM + 1, :]
    o2 = acc_ref[1, :V_DIM, :] / acc_ref[1, V_DIM:V_DIM + 1, :]
    ot = o1 - lam * o2
    y = ot * lax.rsqrt(jnp.mean(ot * ot, axis=0, keepdims=True) + EPS) * g_ref[...]
    y = y * (1.0 - lam_init)
    o_ref[0] = y.T.astype(BF16)


def _attention(cs, lq, g_col, qt, kp, vt, lam_init, batch, seq):
    tq, tk = Q_TILE, K_TILE
    grid = (batch, N_HEADS, seq // tq)
    in_specs = [
        pl.BlockSpec(memory_space=pltpu.SMEM),
        pl.BlockSpec((4, HEAD_DIM), lambda b, h, q: (0, 0)),
        pl.BlockSpec((V_DIM, 1), lambda b, h, q: (0, 0)),
        pl.BlockSpec((1, 2 * HEAD_DIM, tq), lambda b, h, q: (b, h, q)),
        pl.BlockSpec((1, seq, 2 * K_LANES), lambda b, h, q: (b, 0, h)),
        pl.BlockSpec((1, 1, seq // tk, V_ROWS, tk), lambda b, h, q: (b, h, 0, 0, 0)),
    ]
    out_specs = pl.BlockSpec((1, tq, V_DIM), lambda b, h, q: (b, q, h))
    blocks = (_nbytes((seq, 2 * K_LANES), BF16) + _nbytes((seq // tk, V_ROWS, tk), BF16)
              + _nbytes((2 * HEAD_DIM, tq), BF16) + _nbytes((tq, V_DIM), BF16)
              + _nbytes((V_DIM, V7X_LANES), F32) + _nbytes((V7X_SUBLANES, V7X_LANES), F32))
    temps = 2 * _nbytes((V_ROWS, tq), F32) + 12 * _nbytes((tk, tq), F32)
    return pl.pallas_call(
        functools.partial(_attn_kernel, lam_init=lam_init),
        grid=grid, in_specs=in_specs, out_specs=out_specs,
        out_shape=jax.ShapeDtypeStruct((batch, seq, ATTN_WIDTH), BF16),
        scratch_shapes=[pltpu.VMEM((2, V_ROWS, tq), F32)],
        compiler_params=pltpu.CompilerParams(
            dimension_semantics=("arbitrary", "arbitrary", "arbitrary"),
            vmem_limit_bytes=_vmem_limit(blocks, temps)),
        name="attn",
    )(cs, lq, g_col, qt, kp.reshape(batch, seq, -1), vt)


def _mix_kernel(inner_ref, prev_ref, next_ref, bc_ref, o_ref, gates_ref, h_ref,
                cw_ref, bg_ref, wco_ref, wao_ref, wo_ref, g_ref, out_ref, *, tiles_per_seq):
    t = pl.program_id(0) % tiles_per_seq
    x = inner_ref[...]
    tm = x.shape[0]
    prev_row = jnp.where(t == 0, 0.0, prev_ref[V7X_SUBLANES - 1:V7X_SUBLANES, :])
    next_row = jnp.where(t == tiles_per_seq - 1, 0.0, next_ref[0:1, :])
    row = lax.broadcasted_iota(jnp.int32, x.shape, 0)
    up = jnp.where(row == 0, prev_row, pltpu.roll(x, 1, 0))
    dn = jnp.where(row == tm - 1, next_row, pltpu.roll(x, tm - 1, 0))
    conv = up * cw_ref[0:1, :] + x * cw_ref[1:2, :] + dn * cw_ref[2:3, :]
    z = (bc_ref[...] * conv).astype(BF16)
    y_conv = jnp.dot(z, wco_ref[...], preferred_element_type=F32)
    y_attn = jnp.dot(o_ref[...], wao_ref[...], preferred_element_type=F32)
    gate = jax.nn.sigmoid(gates_ref[...] + bg_ref[...])
    merged = (gate[:, :D_MODEL] * y_conv + gate[:, D_MODEL:] * y_attn).astype(BF16)
    m = jnp.dot(merged, wo_ref[...], preferred_element_type=F32)
    out_ref[...] = h_ref[...] + _rms(m, g_ref[...])


def _mix(inner, bc, o, gates, h, cw, bg, wco, wao, wo, g, seq):
    n_tok = h.shape[0]
    tm = ROW_TILE
    hb = tm // V7X_SUBLANES
    n_hb = n_tok // V7X_SUBLANES
    grid = (n_tok // tm,)
    row = lambda i: (i, 0)
    const = lambda i: (0, 0)
    in_specs = [
        pl.BlockSpec((tm, CONV_WIDTH), row),
        pl.BlockSpec((V7X_SUBLANES, CONV_WIDTH), lambda i: (jnp.maximum(i * hb - 1, 0), 0)),
        pl.BlockSpec((V7X_SUBLANES, CONV_WIDTH), lambda i: (jnp.minimum((i + 1) * hb, n_hb - 1), 0)),
        pl.BlockSpec((tm, CONV_WIDTH), row),
        pl.BlockSpec((tm, ATTN_WIDTH), row),
        pl.BlockSpec((tm, 2 * D_MODEL), row),
        pl.BlockSpec((tm, D_MODEL), row),
        pl.BlockSpec((CONV_K, CONV_WIDTH), const),
        pl.BlockSpec((1, 2 * D_MODEL), const),
        pl.BlockSpec((CONV_WIDTH, D_MODEL), const),
        pl.BlockSpec((ATTN_WIDTH, D_MODEL), const),
        pl.BlockSpec((D_MODEL, D_MODEL), const),
        pl.BlockSpec((1, D_MODEL), const),
    ]
    blocks = (2 * _nbytes((tm, CONV_WIDTH), F32) + _nbytes((tm, ATTN_WIDTH), BF16)
              + _nbytes((tm, 2 * D_MODEL), F32) + 2 * _nbytes((tm, D_MODEL), F32)
              + _nbytes((2 * CONV_WIDTH + D_MODEL, D_MODEL), BF16) + 8 * _nbytes((8, 2 * D_MODEL), F32))
    temps = 8 * _nbytes((tm, D_MODEL), F32)
    return pl.pallas_call(
        functools.partial(_mix_kernel, tiles_per_seq=seq // tm),
        grid=grid, in_specs=in_specs, out_specs=pl.BlockSpec((tm, D_MODEL), row),
        out_shape=jax.ShapeDtypeStruct((n_tok, D_MODEL), F32),
        compiler_params=pltpu.CompilerParams(
            dimension_semantics=("arbitrary",), vmem_limit_bytes=_vmem_limit(blocks, temps)),
        name="mix",
    )(inner, inner, inner, bc, o, gates, h, cw, bg, wco, wao, wo, g)


FF_CHUNK = 1024


def _ffn_kernel(h_ref, g_in_ref, w1_ref, w2_ref, g_out_ref, out_ref):
    x = h_ref[...]
    u = _rms(x, g_in_ref[...]).astype(BF16)
    acc = jnp.zeros(x.shape, F32)
    for lo in range(0, D_FF, FF_CHUNK):
        hid = jnp.dot(u, w1_ref[:, lo:lo + FF_CHUNK], preferred_element_type=F32)
        hid = jnp.square(jnp.maximum(hid, 0.0)).astype(BF16)
        acc = acc + jnp.dot(hid, w2_ref[lo:lo + FF_CHUNK, :], preferred_element_type=F32)
    out_ref[...] = x + _rms(acc, g_out_ref[...])


def _ffn(h, g_in, w1, w2, g_out):
    n_tok = h.shape[0]
    tm = ROW_TILE
    row = lambda i: (i, 0)
    const = lambda i: (0, 0)
    in_specs = [
        pl.BlockSpec((tm, D_MODEL), row),
        pl.BlockSpec((1, D_MODEL), const),
        pl.BlockSpec((D_MODEL, D_FF), const),
        pl.BlockSpec((D_FF, D_MODEL), const),
        pl.BlockSpec((1, D_MODEL), const),
    ]
    blocks = 2 * _nbytes((tm, D_MODEL), F32) + 2 * _nbytes((D_MODEL, D_FF), BF16)
    temps = 6 * _nbytes((tm, D_MODEL), F32)
    return pl.pallas_call(
        _ffn_kernel, grid=(n_tok // tm,), in_specs=in_specs,
        out_specs=pl.BlockSpec((tm, D_MODEL), row),
        out_shape=jax.ShapeDtypeStruct((n_tok, D_MODEL), F32),
        compiler_params=pltpu.CompilerParams(
            dimension_semantics=("arbitrary",), vmem_limit_bytes=_vmem_limit(blocks, temps)),
        name="ffn",
    )(h, g_in, w1, w2, g_out)


def _alibi_consts():
    slopes = jnp.exp2(-8.0 * jnp.arange(1, N_HEADS + 1, dtype=F32) / N_HEADS)
    cs = slopes * LOG2E
    x = cs[None, :] * jnp.arange(K_TILE, dtype=F32)[:, None]
    terms, rest = [], x
    for _ in range(N_AUX):
        t = rest.astype(BF16).astype(F32)
        terms.append(t)
        rest = rest - t
    aux = jnp.stack(terms, axis=-1)
    lanes = jnp.zeros((K_TILE, N_HEADS, 2, K_LANES), F32)
    lanes = lanes.at[:, :, :, HEAD_DIM:HEAD_DIM + N_AUX].set(aux[:, :, None, :])
    return cs, lanes.reshape(K_TILE, 2 * N_HEADS * K_LANES)


def _prep_in_weights(w_in):
    cw, aw = CONV_WIDTH, ATTN_WIDTH
    w_conv = w_in[:, :3 * cw]
    w_q = w_in[:, 3 * cw:3 * cw + aw]
    w_k = w_in[:, 3 * cw + aw:3 * cw + 2 * aw]
    w_v = w_in[:, 3 * cw + 2 * aw:3 * cw + 3 * aw]
    w_g = w_in[:, 3 * cw + 3 * aw:]
    w_kp = jnp.pad(w_k.reshape(D_MODEL, 2 * N_HEADS, HEAD_DIM),
                   ((0, 0), (0, 0), (0, K_LANES - HEAD_DIM))).reshape(D_MODEL, -1)
    wnn = jnp.concatenate([w_conv, w_kp, w_g], axis=1).astype(BF16)
    wt = jnp.concatenate([w_q, w_v], axis=1).T.astype(BF16)
    return wnn, wt


def kernel(x, w_in, b_gate, conv_w, w_conv_out, lambda_qk, subln_g, w_attn_out, w_o, norm_g,
           w_ff1, w_ff2):
    batch, seq, _ = x.shape
    depth = w_in.shape[0]
    assert seq % ROW_TILE == 0 and seq % Q_TILE == 0 and K_TILE % Q_TILE == 0
    cs, kaux = _alibi_consts()
    h = x.reshape(batch * seq, D_MODEL)
    for l in range(depth):
        lam_init = 0.8 - 0.6 * math.exp(-0.3 * l)
        wnn, wt = _prep_in_weights(w_in[l])
        inner, bc, gates, kp, qt, vt = _inproj(h, norm_g[l, 0:1], wnn, wt, kaux, batch, seq)
        o = _attention(cs, lambda_qk[l], subln_g[l].reshape(V_DIM, 1), qt, kp, vt,
                       lam_init, batch, seq)
        h = _mix(inner, bc, o.reshape(batch * seq, ATTN_WIDTH), gates, h, conv_w[l],
                 b_gate[l].reshape(1, -1), w_conv_out[l].astype(BF16), w_attn_out[l].astype(BF16),
                 w_o[l].astype(BF16), norm_g[l, 1:2], seq)
        h = _ffn(h, norm_g[l, 2:3], w_ff1[l].astype(BF16), w_ff2[l].astype(BF16), norm_g[l, 3:4])
    return h.reshape(batch, seq, D_MODEL)
```

```python
import functools
import math

import jax
import jax.numpy as jnp
from jax import lax
from jax.experimental import pallas as pl
from jax.experimental.pallas import tpu as pltpu

D_MODEL = 1024
CONV_WIDTH = 512
CONV_K = 3
N_HEADS = 4
HEAD_DIM = 64
V_DIM = 128
ATTN_WIDTH = N_HEADS * V_DIM
D_FF = 4 * D_MODEL
EPS = 1e-6

F32 = jnp.float32
BF16 = jnp.bfloat16

V7X_LANES = 128
V7X_SUBLANES = 8
V7X_BF16_ROWS_PER_VREG = 16
V7X_MXU_DIM = 256
V7X_VMEM_BYTES = 64 * 1024 * 1024

ROW_TILE = 512
Q_TILE = V7X_MXU_DIM
K_TILE = ROW_TILE
K_LANES = 2 * HEAD_DIM
N_AUX = 3
V_ROWS = V_DIM + V7X_BF16_ROWS_PER_VREG
LOG2E = math.log2(math.e)
Q_SCALE = HEAD_DIM ** -0.5 * LOG2E
M_INIT = -1e30

W_NN_COLS = 3 * CONV_WIDTH + 2 * N_HEADS * K_LANES + 2 * D_MODEL
OFF_BC, OFF_CC, OFF_K, OFF_G = 512, 1024, 1536, 1536 + 2 * N_HEADS * K_LANES


def _vmem_limit(block_bytes, temp_bytes):
    need = 2 * block_bytes + temp_bytes
    assert need < V7X_VMEM_BYTES, need
    return int(need)


def _nbytes(shape, dtype):
    return math.prod(shape) * jnp.dtype(dtype).itemsize


def _rms(x, g):
    return x * lax.rsqrt(jnp.mean(x * x, axis=-1, keepdims=True) + EPS) * g


def _inproj_kernel(h_ref, g_ref, wnn_ref, wt_ref, kaux_ref,
                   inner_ref, bc_ref, gates_ref, kp_ref, qt_ref, vt_ref):
    u = _rms(h_ref[...], g_ref[...]).astype(BF16)

    def nn(lo, hi):
        return jnp.dot(u, wnn_ref[:, lo:hi], preferred_element_type=F32)

    inner_ref[...] = nn(OFF_CC, OFF_K) * nn(0, OFF_BC)
    bc_ref[...] = nn(OFF_BC, OFF_CC)
    kp_ref[...] = (nn(OFF_K, OFF_G) + kaux_ref[...]).astype(BF16)
    gates_ref[:, :D_MODEL] = nn(OFF_G, OFF_G + D_MODEL)
    gates_ref[:, D_MODEL:] = nn(OFF_G + D_MODEL, W_NN_COLS)

    qv = lax.dot_general(wt_ref[...], u, (((1,), (1,)), ((), ())),
                         preferred_element_type=F32)
    qt_ref[0] = (qv[:ATTN_WIDTH] * Q_SCALE).astype(BF16)
    tm = u.shape[0]
    ones_rows = jnp.where(
        lax.broadcasted_iota(jnp.int32, (V_ROWS - V_DIM, tm), 0) == 0, 1.0, 0.0).astype(BF16)
    for h in range(N_HEADS):
        lo = ATTN_WIDTH + h * V_DIM
        vt_ref[0, h, 0, :V_DIM, :] = qv[lo:lo + V_DIM].astype(BF16)
        vt_ref[0, h, 0, V_DIM:, :] = ones_rows


def _inproj(h, g, wnn, wt, kaux, batch, seq):
    n_tok = h.shape[0]
    tm = ROW_TILE
    tpb = seq // tm
    grid = (n_tok // tm,)
    row = lambda i: (i, 0)
    const = lambda i: (0, 0)
    out_shape = (
        jax.ShapeDtypeStruct((n_tok, CONV_WIDTH), F32),
        jax.ShapeDtypeStruct((n_tok, CONV_WIDTH), F32),
        jax.ShapeDtypeStruct((n_tok, 2 * D_MODEL), F32),
        jax.ShapeDtypeStruct((n_tok, 2 * N_HEADS * K_LANES), BF16),
        jax.ShapeDtypeStruct((batch, ATTN_WIDTH, seq), BF16),
        jax.ShapeDtypeStruct((batch, N_HEADS, tpb, V_ROWS, tm), BF16),
    )
    in_specs = [
        pl.BlockSpec((tm, D_MODEL), row),
        pl.BlockSpec((1, D_MODEL), const),
        pl.BlockSpec((D_MODEL, W_NN_COLS), const),
        pl.BlockSpec((2 * ATTN_WIDTH, D_MODEL), const),
        pl.BlockSpec((tm, 2 * N_HEADS * K_LANES), const),
    ]
    out_specs = (
        pl.BlockSpec((tm, CONV_WIDTH), row),
        pl.BlockSpec((tm, CONV_WIDTH), row),
        pl.BlockSpec((tm, 2 * D_MODEL), row),
        pl.BlockSpec((tm, 2 * N_HEADS * K_LANES), row),
        pl.BlockSpec((1, ATTN_WIDTH, tm), lambda i: (i // tpb, 0, i % tpb)),
        pl.BlockSpec((1, N_HEADS, 1, V_ROWS, tm), lambda i: (i // tpb, 0, i % tpb, 0, 0)),
    )
    blocks = (_nbytes((tm, D_MODEL), F32) + _nbytes((D_MODEL, W_NN_COLS), BF16)
              + _nbytes((2 * ATTN_WIDTH, D_MODEL), BF16) + _nbytes((tm, 1024), F32)
              + 2 * _nbytes((tm, CONV_WIDTH), F32) + _nbytes((tm, 2 * D_MODEL), F32)
              + _nbytes((tm, 1024), BF16) + _nbytes((ATTN_WIDTH, tm), BF16)
              + _nbytes((N_HEADS, V_ROWS, tm), BF16))
    temps = 4 * _nbytes((tm, D_MODEL), F32) + _nbytes((2 * ATTN_WIDTH, tm), F32)
    return pl.pallas_call(
        _inproj_kernel, grid=grid, in_specs=in_specs, out_specs=out_specs, out_shape=out_shape,
        compiler_params=pltpu.CompilerParams(
            dimension_semantics=("arbitrary",), vmem_limit_bytes=_vmem_limit(blocks, temps)),
        name="inproj",
    )(h, g, wnn, wt, kaux)


def _attn_kernel(cs_ref, lq_ref, g_ref, qt_ref, kp_ref, vt_ref, o_ref,
                 qv_ref, sa_ref, sb_ref, acc_ref, *, lam_init):
    h = pl.program_id(1)
    qi = pl.program_id(2)
    tq, tk = Q_TILE, K_TILE
    n_kb = kp_ref.shape[1] // tk
    c = cs_ref[h]
    q0 = qi * tq
    kd = q0 // tk

    aux_rows = lax.broadcasted_iota(jnp.int32, (K_LANES - HEAD_DIM, tq), 0) < N_AUX
    for v, sign in enumerate((1.0, -1.0, 0.0)):
        for m in range(2):
            qv_ref[v, m, :HEAD_DIM, :] = qt_ref[0, m * HEAD_DIM:(m + 1) * HEAD_DIM, :]
            qv_ref[v, m, HEAD_DIM:, :] = jnp.where(aux_rows, sign, 0.0).astype(BF16)

    ci = c * (q0 + lax.broadcasted_iota(jnp.int32, (1, tq), 1)).astype(F32)
    acc_ref[...] = jnp.zeros_like(acc_ref)

    def scores(kb, variant, s_ref, bias):
        kblk = kp_ref[0, pl.ds(pl.multiple_of(kb * tk, tk), tk), :]
        cms = []
        for m in range(2):
            st = jnp.dot(kblk[:, m * K_LANES:(m + 1) * K_LANES], qv_ref[variant, m],
                         preferred_element_type=F32)
            if bias is not None:
                st = st + bias
            s_ref[m] = st
            cms.append(jnp.max(st, axis=0, keepdims=True))
        return tuple(cms)

    def fold(kb, s_ref, cms, off, ms):
        vblk = vt_ref[0, 0, kb]
        new_ms = []
        for m in range(2):
            m_new = jnp.maximum(ms[m], cms[m] + off)
            alpha = jnp.exp2(ms[m] - m_new)
            p = jnp.exp2(s_ref[m] - (m_new - off)).astype(BF16)
            acc_ref[m] = acc_ref[m] * alpha + jnp.dot(vblk, p, preferred_element_type=F32)
            new_ms.append(m_new)
        return tuple(new_ms)

    ii = lax.broadcasted_iota(jnp.int32, (tk, tq), 1) + (q0 - kd * tk)
    jj = lax.broadcasted_iota(jnp.int32, (tk, tq), 0)
    bias_d = jnp.abs(ii - jj).astype(F32) * (-c)
    cms = scores(kd, 2, sa_ref, bias_d)
    m0 = jnp.full((1, tq), M_INIT, F32)

    def step(i, s_next, s_prev, carry):
        ms, cms_prev, kb_prev, off_prev = carry
        kb = i + (i >= kd).astype(jnp.int32)
        left = kb < kd
        cms_next = scores(kb, jnp.where(left, 0, 1), s_next, None)
        ms = fold(kb_prev, s_prev, cms_prev, off_prev, ms)
        off = jnp.where(left, 1.0, -1.0) * (c * (kb * tk).astype(F32) - ci)
        return ms, cms_next, kb, off

    def pair(j, carry):
        return step(2 * j + 1, sa_ref, sb_ref, step(2 * j, sb_ref, sa_ref, carry))

    assert n_kb % 2 == 0
    carry = lax.fori_loop(0, (n_kb - 2) // 2, pair, ((m0, m0), cms, kd, jnp.zeros((1, tq), F32)))
    ms, cms, kb_last, off_last = step(n_kb - 2, sb_ref, sa_ref, carry)
    fold(kb_last, sb_ref, cms, off_last, ms)

    lq = lq_ref[...]
    lam = (jnp.exp(jnp.sum(lq[0:1] * lq[1:2], axis=-1, keepdims=True))
           - jnp.exp(jnp.sum(lq[2:3] * lq[3:4], axis=-1, keepdims=True)) + lam_init)
    o1 = acc_ref[0, :V_DIM, :] / acc_ref[0, V_DIM:V_DIM + 1, :]
    o2 = acc_ref[1, :V_DIM, :] / acc_ref[1, V_DIM:V_DIM + 1, :]
    ot = o1 - lam * o2
    y = ot * lax.rsqrt(jnp.mean(ot * ot, axis=0, keepdims=True) + EPS) * g_ref[...]
    y = y * (1.0 - lam_init)
    o_ref[0] = y.T.astype(BF16)


def _attention(cs, lq, g_col, qt, kp, vt, lam_init, batch, seq):
    tq, tk = Q_TILE, K_TILE
    grid = (batch, N_HEADS, seq // tq)
    in_specs = [
        pl.BlockSpec(memory_space=pltpu.SMEM),
        pl.BlockSpec((4, HEAD_DIM), lambda b, h, q: (0, 0)),
        pl.BlockSpec((V_DIM, 1), lambda b, h, q: (0, 0)),
        pl.BlockSpec((1, 2 * HEAD_DIM, tq), lambda b, h, q: (b, h, q)),
        pl.BlockSpec((1, seq, 2 * K_LANES), lambda b, h, q: (b, 0, h)),
        pl.BlockSpec((1, 1, seq // tk, V_ROWS, tk), lambda b, h, q: (b, h, 0, 0, 0)),
    ]
    out_specs = pl.BlockSpec((1, tq, V_DIM), lambda b, h, q: (b, q, h))
    blocks = (_nbytes((seq, 2 * K_LANES), BF16) + _nbytes((seq // tk, V_ROWS, tk), BF16)
              + _nbytes((2 * HEAD_DIM, tq), BF16) + _nbytes((tq, V_DIM), BF16)
              + _nbytes((V_DIM, V7X_LANES), F32) + _nbytes((V7X_SUBLANES, V7X_LANES), F32))
    temps = 2 * _nbytes((V_ROWS, tq), F32) + 12 * _nbytes((tk, tq), F32)
    return pl.pallas_call(
        functools.partial(_attn_kernel, lam_init=lam_init),
        grid=grid, in_specs=in_specs, out_specs=out_specs,
        out_shape=jax.ShapeDtypeStruct((batch, seq, ATTN_WIDTH), BF16),
        scratch_shapes=[pltpu.VMEM((3, 2, K_LANES, tq), BF16),
                        pltpu.VMEM((2, tk, tq), F32),
                        pltpu.VMEM((2, tk, tq), F32),
                        pltpu.VMEM((2, V_ROWS, tq), F32)],
        compiler_params=pltpu.CompilerParams(
            dimension_semantics=("arbitrary", "arbitrary", "arbitrary"),
            vmem_limit_bytes=_vmem_limit(blocks, temps)),
        name="attn",
    )(cs, lq, g_col, qt, kp.reshape(batch, seq, -1), vt)


def _mix_kernel(inner_ref, prev_ref, next_ref, bc_ref, o_ref, gates_ref, h_ref,
                cw_ref, bg_ref, wco_ref, wao_ref, wo_ref, g_ref, out_ref, *, tiles_per_seq):
    t = pl.program_id(0) % tiles_per_seq
    x = inner_ref[...]
    tm = x.shape[0]
    prev_row = jnp.where(t == 0, 0.0, prev_ref[V7X_SUBLANES - 1:V7X_SUBLANES, :])
    next_row = jnp.where(t == tiles_per_seq - 1, 0.0, next_ref[0:1, :])
    row = lax.broadcasted_iota(jnp.int32, x.shape, 0)
    up = jnp.where(row == 0, prev_row, pltpu.roll(x, 1, 0))
    dn = jnp.where(row == tm - 1, next_row, pltpu.roll(x, tm - 1, 0))
    conv = up * cw_ref[0:1, :] + x * cw_ref[1:2, :] + dn * cw_ref[2:3, :]
    z = (bc_ref[...] * conv).astype(BF16)
    y_conv = jnp.dot(z, wco_ref[...], preferred_element_type=F32)
    y_attn = jnp.dot(o_ref[...], wao_ref[...], preferred_element_type=F32)
    gate = jax.nn.sigmoid(gates_ref[...] + bg_ref[...])
    merged = (gate[:, :D_MODEL] * y_conv + gate[:, D_MODEL:] * y_attn).astype(BF16)
    m = jnp.dot(merged, wo_ref[...], preferred_element_type=F32)
    out_ref[...] = h_ref[...] + _rms(m, g_ref[...])


def _mix(inner, bc, o, gates, h, cw, bg, wco, wao, wo, g, seq):
    n_tok = h.shape[0]
    tm = ROW_TILE
    hb = tm // V7X_SUBLANES
    n_hb = n_tok // V7X_SUBLANES
    grid = (n_tok // tm,)
    row = lambda i: (i, 0)
    const = lambda i: (0, 0)
    in_specs = [
        pl.BlockSpec((tm, CONV_WIDTH), row),
        pl.BlockSpec((V7X_SUBLANES, CONV_WIDTH), lambda i: (jnp.maximum(i * hb - 1, 0), 0)),
        pl.BlockSpec((V7X_SUBLANES, CONV_WIDTH), lambda i: (jnp.minimum((i + 1) * hb, n_hb - 1), 0)),
        pl.BlockSpec((tm, CONV_WIDTH), row),
        pl.BlockSpec((tm, ATTN_WIDTH), row),
        pl.BlockSpec((tm, 2 * D_MODEL), row),
        pl.BlockSpec((tm, D_MODEL), row),
        pl.BlockSpec((CONV_K, CONV_WIDTH), const),
        pl.BlockSpec((1, 2 * D_MODEL), const),
        pl.BlockSpec((CONV_WIDTH, D_MODEL), const),
        pl.BlockSpec((ATTN_WIDTH, D_MODEL), const),
        pl.BlockSpec((D_MODEL, D_MODEL), const),
        pl.BlockSpec((1, D_MODEL), const),
    ]
    blocks = (2 * _nbytes((tm, CONV_WIDTH), F32) + _nbytes((tm, ATTN_WIDTH), BF16)
              + _nbytes((tm, 2 * D_MODEL), F32) + 2 * _nbytes((tm, D_MODEL), F32)
              + _nbytes((2 * CONV_WIDTH + D_MODEL, D_MODEL), BF16) + 8 * _nbytes((8, 2 * D_MODEL), F32))
    temps = 8 * _nbytes((tm, D_MODEL), F32)
    return pl.pallas_call(
        functools.partial(_mix_kernel, tiles_per_seq=seq // tm),
        grid=grid, in_specs=in_specs, out_specs=pl.BlockSpec((tm, D_MODEL), row),
        out_shape=jax.ShapeDtypeStruct((n_tok, D_MODEL), F32),
        compiler_params=pltpu.CompilerParams(
            dimension_semantics=("arbitrary",), vmem_limit_bytes=_vmem_limit(blocks, temps)),
        name="mix",
    )(inner, inner, inner, bc, o, gates, h, cw, bg, wco, wao, wo, g)


FF_CHUNK = 1024


def _ffn_kernel(h_ref, g_in_ref, w1_ref, w2_ref, g_out_ref, out_ref):
    x = h_ref[...]
    u = _rms(x, g_in_ref[...]).astype(BF16)
    acc = jnp.zeros(x.shape, F32)
    for lo in range(0, D_FF, FF_CHUNK):
        hid = jnp.dot(u, w1_ref[:, lo:lo + FF_CHUNK], preferred_element_type=F32)
        hid = jnp.square(jnp.maximum(hid, 0.0)).astype(BF16)
        acc = acc + jnp.dot(hid, w2_ref[lo:lo + FF_CHUNK, :], preferred_element_type=F32)
    out_ref[...] = x + _rms(acc, g_out_ref[...])


def _ffn(h, g_in, w1, w2, g_out):
    n_tok = h.shape[0]
    tm = ROW_TILE
    row = lambda i: (i, 0)
    const = lambda i: (0, 0)
    in_specs = [
        pl.BlockSpec((tm, D_MODEL), row),
        pl.BlockSpec((1, D_MODEL), const),
        pl.BlockSpec((D_MODEL, D_FF), const),
        pl.BlockSpec((D_FF, D_MODEL), const),
        pl.BlockSpec((1, D_MODEL), const),
    ]
    blocks = 2 * _nbytes((tm, D_MODEL), F32) + 2 * _nbytes((D_MODEL, D_FF), BF16)
    temps = 6 * _nbytes((tm, D_MODEL), F32)
    return pl.pallas_call(
        _ffn_kernel, grid=(n_tok // tm,), in_specs=in_specs,
        out_specs=pl.BlockSpec((tm, D_MODEL), row),
        out_shape=jax.ShapeDtypeStruct((n_tok, D_MODEL), F32),
        compiler_params=pltpu.CompilerParams(
            dimension_semantics=("arbitrary",), vmem_limit_bytes=_vmem_limit(blocks, temps)),
        name="ffn",
    )(h, g_in, w1, w2, g_out)


def _alibi_consts():
    slopes = jnp.exp2(-8.0 * jnp.arange(1, N_HEADS + 1, dtype=F32) / N_HEADS)
    cs = slopes * LOG2E
    x = cs[None, :] * jnp.arange(K_TILE, dtype=F32)[:, None]
    terms, rest = [], x
    for _ in range(N_AUX):
        t = rest.astype(BF16).astype(F32)
        terms.append(t)
        rest = rest - t
    aux = jnp.stack(terms, axis=-1)
    lanes = jnp.zeros((K_TILE, N_HEADS, 2, K_LANES), F32)
    lanes = lanes.at[:, :, :, HEAD_DIM:HEAD_DIM + N_AUX].set(aux[:, :, None, :])
    return cs, lanes.reshape(K_TILE, 2 * N_HEADS * K_LANES)


def _prep_in_weights(w_in):
    cw, aw = CONV_WIDTH, ATTN_WIDTH
    w_conv = w_in[:, :3 * cw]
    w_q = w_in[:, 3 * cw:3 * cw + aw]
    w_k = w_in[:, 3 * cw + aw:3 * cw + 2 * aw]
    w_v = w_in[:, 3 * cw + 2 * aw:3 * cw + 3 * aw]
    w_g = w_in[:, 3 * cw + 3 * aw:]
    w_kp = jnp.pad(w_k.reshape(D_MODEL, 2 * N_HEADS, HEAD_DIM),
                   ((0, 0), (0, 0), (0, K_LANES - HEAD_DIM))).reshape(D_MODEL, -1)
    wnn = jnp.concatenate([w_conv, w_kp, w_g], axis=1).astype(BF16)
    wt = jnp.concatenate([w_q, w_v], axis=1).T.astype(BF16)
    return wnn, wt


def kernel(x, w_in, b_gate, conv_w, w_conv_out, lambda_qk, subln_g, w_attn_out, w_o, norm_g,
           w_ff1, w_ff2):
    batch, seq, _ = x.shape
    depth = w_in.shape[0]
    assert seq % ROW_TILE == 0 and seq % Q_TILE == 0 and K_TILE % Q_TILE == 0
    cs, kaux = _alibi_consts()
    h = x.reshape(batch * seq, D_MODEL)
    for l in range(depth):
        lam_init = 0.8 - 0.6 * math.exp(-0.3 * l)
        wnn, wt = _prep_in_weights(w_in[l])
        inner, bc, gates, kp, qt, vt = _inproj(h, norm_g[l, 0:1], wnn, wt, kaux, batch, seq)
        o = _attention(cs, lambda_qk[l], subln_g[l].reshape(V_DIM, 1), qt, kp, vt,
                       lam_init, batch, seq)
        h = _mix(inner, bc, o.reshape(batch * seq, ATTN_WIDTH), gates, h, conv_w[l],
                 b_gate[l].reshape(1, -1), w_conv_out[l].astype(BF16), w_attn_out[l].astype(BF16),
                 w_o[l].astype(BF16), norm_g[l, 1:2], seq)
        h = _ffn(h, norm_g[l, 2:3], w_ff1[l].astype(BF16), w_ff2[l].astype(BF16), norm_g[l, 3:4])
    return h.reshape(batch, seq, D_MODEL)
```

```python
import functools
import math

import jax
import jax.numpy as jnp
from jax import lax
from jax.experimental import pallas as pl
from jax.experimental.pallas import tpu as pltpu

D_MODEL = 1024
CONV_WIDTH = 512
CONV_K = 3
N_HEADS = 4
HEAD_DIM = 64
V_DIM = 128
ATTN_WIDTH = N_HEADS * V_DIM
D_FF = 4 * D_MODEL
EPS = 1e-6

F32 = jnp.float32
BF16 = jnp.bfloat16

V7X_LANES = 128
V7X_SUBLANES = 8
V7X_BF16_ROWS_PER_VREG = 16
V7X_MXU_DIM = 256
V7X_VMEM_BYTES = 64 * 1024 * 1024

ROW_TILE = 512
Q_TILE = V7X_MXU_DIM
K_TILE = ROW_TILE
K_LANES = 2 * HEAD_DIM
N_AUX = 3
V_ROWS = V_DIM + V7X_BF16_ROWS_PER_VREG
LOG2E = math.log2(math.e)
Q_SCALE = HEAD_DIM ** -0.5 * LOG2E
M_INIT = -1e30
STEPS_PER_ITER = 6

W_NN_COLS = 3 * CONV_WIDTH + 2 * N_HEADS * K_LANES + 2 * D_MODEL
OFF_BC, OFF_CC, OFF_K, OFF_G = 512, 1024, 1536, 1536 + 2 * N_HEADS * K_LANES


def _vmem_limit(block_bytes, temp_bytes):
    need = 2 * block_bytes + temp_bytes
    assert need < V7X_VMEM_BYTES, need
    return int(need)


def _nbytes(shape, dtype):
    return math.prod(shape) * jnp.dtype(dtype).itemsize


def _rms(x, g):
    return x * lax.rsqrt(jnp.mean(x * x, axis=-1, keepdims=True) + EPS) * g


def _inproj_kernel(h_ref, g_ref, wnn_ref, wt_ref, kaux_ref,
                   inner_ref, bc_ref, gates_ref, kp_ref, qt_ref, vt_ref):
    u = _rms(h_ref[...], g_ref[...]).astype(BF16)

    def nn(lo, hi):
        return jnp.dot(u, wnn_ref[:, lo:hi], preferred_element_type=F32)

    inner_ref[...] = nn(OFF_CC, OFF_K) * nn(0, OFF_BC)
    bc_ref[...] = nn(OFF_BC, OFF_CC)
    kp_ref[...] = (nn(OFF_K, OFF_G) + kaux_ref[...]).astype(BF16)
    gates_ref[:, :D_MODEL] = nn(OFF_G, OFF_G + D_MODEL)
    gates_ref[:, D_MODEL:] = nn(OFF_G + D_MODEL, W_NN_COLS)

    qv = lax.dot_general(wt_ref[...], u, (((1,), (1,)), ((), ())),
                         preferred_element_type=F32)
    qt_ref[0] = (qv[:ATTN_WIDTH] * Q_SCALE).astype(BF16)
    tm = u.shape[0]
    ones_rows = jnp.where(
        lax.broadcasted_iota(jnp.int32, (V_ROWS - V_DIM, tm), 0) == 0, 1.0, 0.0).astype(BF16)
    for h in range(N_HEADS):
        lo = ATTN_WIDTH + h * V_DIM
        vt_ref[0, h, 0, :V_DIM, :] = qv[lo:lo + V_DIM].astype(BF16)
        vt_ref[0, h, 0, V_DIM:, :] = ones_rows


def _inproj(h, g, wnn, wt, kaux, batch, seq):
    n_tok = h.shape[0]
    tm = ROW_TILE
    tpb = seq // tm
    grid = (n_tok // tm,)
    row = lambda i: (i, 0)
    const = lambda i: (0, 0)
    out_shape = (
        jax.ShapeDtypeStruct((n_tok, CONV_WIDTH), F32),
        jax.ShapeDtypeStruct((n_tok, CONV_WIDTH), F32),
        jax.ShapeDtypeStruct((n_tok, 2 * D_MODEL), F32),
        jax.ShapeDtypeStruct((n_tok, 2 * N_HEADS * K_LANES), BF16),
        jax.ShapeDtypeStruct((batch, ATTN_WIDTH, seq), BF16),
        jax.ShapeDtypeStruct((batch, N_HEADS, tpb, V_ROWS, tm), BF16),
    )
    in_specs = [
        pl.BlockSpec((tm, D_MODEL), row),
        pl.BlockSpec((1, D_MODEL), const),
        pl.BlockSpec((D_MODEL, W_NN_COLS), const),
        pl.BlockSpec((2 * ATTN_WIDTH, D_MODEL), const),
        pl.BlockSpec((tm, 2 * N_HEADS * K_LANES), const),
    ]
    out_specs = (
        pl.BlockSpec((tm, CONV_WIDTH), row),
        pl.BlockSpec((tm, CONV_WIDTH), row),
        pl.BlockSpec((tm, 2 * D_MODEL), row),
        pl.BlockSpec((tm, 2 * N_HEADS * K_LANES), row),
        pl.BlockSpec((1, ATTN_WIDTH, tm), lambda i: (i // tpb, 0, i % tpb)),
        pl.BlockSpec((1, N_HEADS, 1, V_ROWS, tm), lambda i: (i // tpb, 0, i % tpb, 0, 0)),
    )
    blocks = (_nbytes((tm, D_MODEL), F32) + _nbytes((D_MODEL, W_NN_COLS), BF16)
              + _nbytes((2 * ATTN_WIDTH, D_MODEL), BF16) + _nbytes((tm, 1024), F32)
              + 2 * _nbytes((tm, CONV_WIDTH), F32) + _nbytes((tm, 2 * D_MODEL), F32)
              + _nbytes((tm, 1024), BF16) + _nbytes((ATTN_WIDTH, tm), BF16)
              + _nbytes((N_HEADS, V_ROWS, tm), BF16))
    temps = 4 * _nbytes((tm, D_MODEL), F32) + _nbytes((2 * ATTN_WIDTH, tm), F32)
    return pl.pallas_call(
        _inproj_kernel, grid=grid, in_specs=in_specs, out_specs=out_specs, out_shape=out_shape,
        compiler_params=pltpu.CompilerParams(
            dimension_semantics=("arbitrary",), vmem_limit_bytes=_vmem_limit(blocks, temps)),
        name="inproj",
    )(h, g, wnn, wt, kaux)


def _attn_kernel(cs_ref, lq_ref, g_ref, dist_ref, qt_ref, kp_ref, vt_ref, o_ref,
                 qv_ref, sa_ref, sb_ref, acc_ref, *, lam_init):
    h = pl.program_id(1)
    qi = pl.program_id(2)
    tq, tk = Q_TILE, K_TILE
    n_kb = kp_ref.shape[1] // tk
    c = cs_ref[h]
    q0 = qi * tq
    kd = q0 // tk

    aux_rows = lax.broadcasted_iota(jnp.int32, (K_LANES - HEAD_DIM, tq), 0) < N_AUX
    for v, sign in enumerate((1.0, -1.0, 0.0)):
        for m in range(2):
            qv_ref[v, m, :HEAD_DIM, :] = qt_ref[0, m * HEAD_DIM:(m + 1) * HEAD_DIM, :]
            qv_ref[v, m, HEAD_DIM:, :] = jnp.where(aux_rows, sign, 0.0).astype(BF16)

    ci = c * (q0 + lax.broadcasted_iota(jnp.int32, (1, tq), 1)).astype(F32)
    acc_ref[...] = jnp.zeros_like(acc_ref)

    def scores(kb, variant, s_ref, bias):
        kblk = kp_ref[0, pl.ds(pl.multiple_of(kb * tk, tk), tk), :]
        cms = []
        for m in range(2):
            st = jnp.dot(kblk[:, m * K_LANES:(m + 1) * K_LANES], qv_ref[variant, m],
                         preferred_element_type=F32)
            if bias is not None:
                st = st + bias
            s_ref[m] = st
            cms.append(jnp.max(st, axis=0, keepdims=True))
        return tuple(cms)

    def fold(kb, s_ref, cms, off, ms):
        vblk = vt_ref[0, 0, kb]
        new_ms = []
        for m in range(2):
            m_new = jnp.maximum(ms[m], cms[m] + off)
            alpha = jnp.exp2(ms[m] - m_new)
            p = jnp.exp2(s_ref[m] - (m_new - off)).astype(BF16)
            acc_ref[m] = acc_ref[m] * alpha + jnp.dot(vblk, p, preferred_element_type=F32)
            new_ms.append(m_new)
        return tuple(new_ms)

    cms = scores(kd, 2, sa_ref, dist_ref[qi % (tk // tq)] * (-c))
    m0 = jnp.full((1, tq), M_INIT, F32)

    def step(i, s_next, s_prev, carry):
        ms, cms_prev, kb_prev, off_prev = carry
        kb = i + (i >= kd).astype(jnp.int32)
        left = kb < kd
        cms_next = scores(kb, jnp.where(left, 0, 1), s_next, None)
        ms = fold(kb_prev, s_prev, cms_prev, off_prev, ms)
        off = jnp.where(left, 1.0, -1.0) * (c * (kb * tk).astype(F32) - ci)
        return ms, cms_next, kb, off

    bufs = (sa_ref, sb_ref)

    def steps(first, n, carry):
        for u in range(n):
            carry = step(first + u, bufs[(u + 1) % 2], bufs[u % 2], carry)
        return carry

    n_steps = n_kb - 1
    n_iter = n_steps // STEPS_PER_ITER
    carry = lax.fori_loop(
        0, n_iter, lambda j, carry: steps(j * STEPS_PER_ITER, STEPS_PER_ITER, carry),
        ((m0, m0), cms, kd, jnp.zeros((1, tq), F32)))
    n_tail = n_steps - n_iter * STEPS_PER_ITER
    ms, cms, kb_last, off_last = steps(jnp.int32(n_iter * STEPS_PER_ITER), n_tail, carry)
    fold(kb_last, bufs[n_tail % 2], cms, off_last, ms)

    lq = lq_ref[...]
    lam = (jnp.exp(jnp.sum(lq[0:1] * lq[1:2], axis=-1, keepdims=True))
           - jnp.exp(jnp.sum(lq[2:3] * lq[3:4], axis=-1, keepdims=True)) + lam_init)
    o1 = acc_ref[0, :V_DIM, :] / acc_ref[0, V_DIM:V_DIM + 1, :]
    o2 = acc_ref[1, :V_DIM, :] / acc_ref[1, V_DIM:V_DIM + 1, :]
    ot = o1 - lam * o2
    y = ot * lax.rsqrt(jnp.mean(ot * ot, axis=0, keepdims=True) + EPS) * g_ref[...]
    y = y * (1.0 - lam_init)
    o_ref[0] = y.T.astype(BF16)


def _attention(cs, lq, g_col, dist, qt, kp, vt, lam_init, batch, seq):
    tq, tk = Q_TILE, K_TILE
    grid = (batch, N_HEADS, seq // tq)
    in_specs = [
        pl.BlockSpec(memory_space=pltpu.SMEM),
        pl.BlockSpec((4, HEAD_DIM), lambda b, h, q: (0, 0)),
        pl.BlockSpec((V_DIM, 1), lambda b, h, q: (0, 0)),
        pl.BlockSpec((tk // tq, tk, tq), lambda b, h, q: (0, 0, 0)),
        pl.BlockSpec((1, 2 * HEAD_DIM, tq), lambda b, h, q: (b, h, q)),
        pl.BlockSpec((1, seq, 2 * K_LANES), lambda b, h, q: (b, 0, h)),
        pl.BlockSpec((1, 1, seq // tk, V_ROWS, tk), lambda b, h, q: (b, h, 0, 0, 0)),
    ]
    out_specs = pl.BlockSpec((1, tq, V_DIM), lambda b, h, q: (b, q, h))
    blocks = (_nbytes((seq, 2 * K_LANES), BF16) + _nbytes((seq // tk, V_ROWS, tk), BF16)
              + _nbytes((2 * HEAD_DIM, tq), BF16) + _nbytes((tq, V_DIM), BF16)
              + _nbytes((V_DIM, V7X_LANES), F32) + _nbytes((V7X_SUBLANES, V7X_LANES), F32)
              + _nbytes((tk // tq, tk, tq), F32))
    temps = 2 * _nbytes((V_ROWS, tq), F32) + 12 * _nbytes((tk, tq), F32)
    return pl.pallas_call(
        functools.partial(_attn_kernel, lam_init=lam_init),
        grid=grid, in_specs=in_specs, out_specs=out_specs,
        out_shape=jax.ShapeDtypeStruct((batch, seq, ATTN_WIDTH), BF16),
        scratch_shapes=[pltpu.VMEM((3, 2, K_LANES, tq), BF16),
                        pltpu.VMEM((2, tk, tq), F32),
                        pltpu.VMEM((2, tk, tq), F32),
                        pltpu.VMEM((2, V_ROWS, tq), F32)],
        compiler_params=pltpu.CompilerParams(
            dimension_semantics=("arbitrary", "arbitrary", "arbitrary"),
            vmem_limit_bytes=_vmem_limit(blocks, temps)),
        name="attn",
    )(cs, lq, g_col, dist, qt, kp.reshape(batch, seq, -1), vt)


def _mix_kernel(inner_ref, prev_ref, next_ref, bc_ref, o_ref, gates_ref, h_ref,
                cw_ref, bg_ref, wco_ref, wao_ref, wo_ref, g_ref, out_ref, *, tiles_per_seq):
    t = pl.program_id(0) % tiles_per_seq
    x = inner_ref[...]
    tm = x.shape[0]
    prev_row = jnp.where(t == 0, 0.0, prev_ref[V7X_SUBLANES - 1:V7X_SUBLANES, :])
    next_row = jnp.where(t == tiles_per_seq - 1, 0.0, next_ref[0:1, :])
    row = lax.broadcasted_iota(jnp.int32, x.shape, 0)
    up = jnp.where(row == 0, prev_row, pltpu.roll(x, 1, 0))
    dn = jnp.where(row == tm - 1, next_row, pltpu.roll(x, tm - 1, 0))
    conv = up * cw_ref[0:1, :] + x * cw_ref[1:2, :] + dn * cw_ref[2:3, :]
    z = (bc_ref[...] * conv).astype(BF16)
    y_conv = jnp.dot(z, wco_ref[...], preferred_element_type=F32)
    y_attn = jnp.dot(o_ref[...], wao_ref[...], preferred_element_type=F32)
    gate = jax.nn.sigmoid(gates_ref[...] + bg_ref[...])
    merged = (gate[:, :D_MODEL] * y_conv + gate[:, D_MODEL:] * y_attn).astype(BF16)
    m = jnp.dot(merged, wo_ref[...], preferred_element_type=F32)
    out_ref[...] = h_ref[...] + _rms(m, g_ref[...])


def _mix(inner, bc, o, gates, h, cw, bg, wco, wao, wo, g, seq):
    n_tok = h.shape[0]
    tm = ROW_TILE
    hb = tm // V7X_SUBLANES
    n_hb = n_tok // V7X_SUBLANES
    grid = (n_tok // tm,)
    row = lambda i: (i, 0)
    const = lambda i: (0, 0)
    in_specs = [
        pl.BlockSpec((tm, CONV_WIDTH), row),
        pl.BlockSpec((V7X_SUBLANES, CONV_WIDTH), lambda i: (jnp.maximum(i * hb - 1, 0), 0)),
        pl.BlockSpec((V7X_SUBLANES, CONV_WIDTH), lambda i: (jnp.minimum((i + 1) * hb, n_hb - 1), 0)),
        pl.BlockSpec((tm, CONV_WIDTH), row),
        pl.BlockSpec((tm, ATTN_WIDTH), row),
        pl.BlockSpec((tm, 2 * D_MODEL), row),
        pl.BlockSpec((tm, D_MODEL), row),
        pl.BlockSpec((CONV_K, CONV_WIDTH), const),
        pl.BlockSpec((1, 2 * D_MODEL), const),
        pl.BlockSpec((CONV_WIDTH, D_MODEL), const),
        pl.BlockSpec((ATTN_WIDTH, D_MODEL), const),
        pl.BlockSpec((D_MODEL, D_MODEL), const),
        pl.BlockSpec((1, D_MODEL), const),
    ]
    blocks = (2 * _nbytes((tm, CONV_WIDTH), F32) + _nbytes((tm, ATTN_WIDTH), BF16)
              + _nbytes((tm, 2 * D_MODEL), F32) + 2 * _nbytes((tm, D_MODEL), F32)
              + _nbytes((2 * CONV_WIDTH + D_MODEL, D_MODEL), BF16) + 8 * _nbytes((8, 2 * D_MODEL), F32))
    temps = 8 * _nbytes((tm, D_MODEL), F32)
    return pl.pallas_call(
        functools.partial(_mix_kernel, tiles_per_seq=seq // tm),
        grid=grid, in_specs=in_specs, out_specs=pl.BlockSpec((tm, D_MODEL), row),
        out_shape=jax.ShapeDtypeStruct((n_tok, D_MODEL), F32),
        compiler_params=pltpu.CompilerParams(
            dimension_semantics=("arbitrary",), vmem_limit_bytes=_vmem_limit(blocks, temps)),
        name="mix",
    )(inner, inner, inner, bc, o, gates, h, cw, bg, wco, wao, wo, g)


FF_CHUNK = 1024


def _ffn_kernel(h_ref, g_in_ref, w1_ref, w2_ref, g_out_ref, out_ref):
    x = h_ref[...]
    u = _rms(x, g_in_ref[...]).astype(BF16)
    acc = jnp.zeros(x.shape, F32)
    for lo in range(0, D_FF, FF_CHUNK):
        hid = jnp.dot(u, w1_ref[:, lo:lo + FF_CHUNK], preferred_element_type=F32)
        hid = jnp.square(jnp.maximum(hid, 0.0)).astype(BF16)
        acc = acc + jnp.dot(hid, w2_ref[lo:lo + FF_CHUNK, :], preferred_element_type=F32)
    out_ref[...] = x + _rms(acc, g_out_ref[...])


def _ffn(h, g_in, w1, w2, g_out):
    n_tok = h.shape[0]
    tm = ROW_TILE
    row = lambda i: (i, 0)
    const = lambda i: (0, 0)
    in_specs = [
        pl.BlockSpec((tm, D_MODEL), row),
        pl.BlockSpec((1, D_MODEL), const),
        pl.BlockSpec((D_MODEL, D_FF), const),
        pl.BlockSpec((D_FF, D_MODEL), const),
        pl.BlockSpec((1, D_MODEL), const),
    ]
    blocks = 2 * _nbytes((tm, D_MODEL), F32) + 2 * _nbytes((D_MODEL, D_FF), BF16)
    temps = 6 * _nbytes((tm, D_MODEL), F32)
    return pl.pallas_call(
        _ffn_kernel, grid=(n_tok // tm,), in_specs=in_specs,
        out_specs=pl.BlockSpec((tm, D_MODEL), row),
        out_shape=jax.ShapeDtypeStruct((n_tok, D_MODEL), F32),
        compiler_params=pltpu.CompilerParams(
            dimension_semantics=("arbitrary",), vmem_limit_bytes=_vmem_limit(blocks, temps)),
        name="ffn",
    )(h, g_in, w1, w2, g_out)


def _alibi_consts():
    slopes = jnp.exp2(-8.0 * jnp.arange(1, N_HEADS + 1, dtype=F32) / N_HEADS)
    cs = slopes * LOG2E
    x = cs[None, :] * jnp.arange(K_TILE, dtype=F32)[:, None]
    terms, rest = [], x
    for _ in range(N_AUX):
        t = rest.astype(BF16).astype(F32)
        terms.append(t)
        rest = rest - t
    aux = jnp.stack(terms, axis=-1)
    lanes = jnp.zeros((K_TILE, N_HEADS, 2, K_LANES), F32)
    lanes = lanes.at[:, :, :, HEAD_DIM:HEAD_DIM + N_AUX].set(aux[:, :, None, :])
    n_pos = K_TILE // Q_TILE
    qpos = jnp.arange(Q_TILE)[None, None, :] + Q_TILE * jnp.arange(n_pos)[:, None, None]
    dist = jnp.abs(qpos - jnp.arange(K_TILE)[None, :, None]).astype(F32)
    return cs, lanes.reshape(K_TILE, 2 * N_HEADS * K_LANES), dist


def _prep_in_weights(w_in):
    cw, aw = CONV_WIDTH, ATTN_WIDTH
    w_conv = w_in[:, :3 * cw]
    w_q = w_in[:, 3 * cw:3 * cw + aw]
    w_k = w_in[:, 3 * cw + aw:3 * cw + 2 * aw]
    w_v = w_in[:, 3 * cw + 2 * aw:3 * cw + 3 * aw]
    w_g = w_in[:, 3 * cw + 3 * aw:]
    w_kp = jnp.pad(w_k.reshape(D_MODEL, 2 * N_HEADS, HEAD_DIM),
                   ((0, 0), (0, 0), (0, K_LANES - HEAD_DIM))).reshape(D_MODEL, -1)
    wnn = jnp.concatenate([w_conv, w_kp, w_g], axis=1).astype(BF16)
    wt = jnp.concatenate([w_q, w_v], axis=1).T.astype(BF16)
    return wnn, wt


def kernel(x, w_in, b_gate, conv_w, w_conv_out, lambda_qk, subln_g, w_attn_out, w_o, norm_g,
           w_ff1, w_ff2):
    batch, seq, _ = x.shape
    depth = w_in.shape[0]
    assert seq % ROW_TILE == 0 and seq % Q_TILE == 0 and K_TILE % Q_TILE == 0
    cs, kaux, dist = _alibi_consts()
    h = x.reshape(batch * seq, D_MODEL)
    for l in range(depth):
        lam_init = 0.8 - 0.6 * math.exp(-0.3 * l)
        wnn, wt = _prep_in_weights(w_in[l])
        inner, bc, gates, kp, qt, vt = _inproj(h, norm_g[l, 0:1], wnn, wt, kaux, batch, seq)
        o = _attention(cs, lambda_qk[l], subln_g[l].reshape(V_DIM, 1), dist, qt, kp, vt,
                       lam_init, batch, seq)
        h = _mix(inner, bc, o.reshape(batch * seq, ATTN_WIDTH), gates, h, conv_w[l],
                 b_gate[l].reshape(1, -1), w_conv_out[l].astype(BF16), w_attn_out[l].astype(BF16),
                 w_o[l].astype(BF16), norm_g[l, 1:2], seq)
        h = _ffn(h, norm_g[l, 2:3], w_ff1[l].astype(BF16), w_ff2[l].astype(BF16), norm_g[l, 3:4])
    return h.reshape(batch, seq, D_MODEL)
```

```python
import functools
import math

import jax
import jax.numpy as jnp
from jax import lax
from jax.experimental import pallas as pl
from jax.experimental.pallas import tpu as pltpu

D_MODEL = 1024
CONV_WIDTH = 512
CONV_K = 3
N_HEADS = 4
HEAD_DIM = 64
V_DIM = 128
ATTN_WIDTH = N_HEADS * V_DIM
D_FF = 4 * D_MODEL
EPS = 1e-6

F32 = jnp.float32
BF16 = jnp.bfloat16

V7X_LANES = 128
V7X_SUBLANES = 8
V7X_BF16_ROWS_PER_VREG = 16
V7X_MXU_DIM = 256
V7X_VMEM_BYTES = 64 * 1024 * 1024

ROW_TILE = 512
Q_TILE = V7X_MXU_DIM
K_TILE = ROW_TILE
K_LANES = 2 * HEAD_DIM
N_AUX = 3
V_ROWS = V_DIM + V7X_BF16_ROWS_PER_VREG
LOG2E = math.log2(math.e)
Q_SCALE = HEAD_DIM ** -0.5 * LOG2E
M_INIT = -1e30
STEPS_PER_ITER = 6
SKIP_MARGIN = 152.0
NORM_SLACK = 1.01

W_NN_COLS = 3 * CONV_WIDTH + 2 * N_HEADS * K_LANES + 2 * D_MODEL
OFF_BC, OFF_CC, OFF_K, OFF_G = 512, 1024, 1536, 1536 + 2 * N_HEADS * K_LANES


def _vmem_limit(block_bytes, temp_bytes):
    need = 2 * block_bytes + temp_bytes
    assert need < V7X_VMEM_BYTES, need
    return int(need)


def _nbytes(shape, dtype):
    return math.prod(shape) * jnp.dtype(dtype).itemsize


def _rms(x, g):
    return x * lax.rsqrt(jnp.mean(x * x, axis=-1, keepdims=True) + EPS) * g


def _inproj_kernel(h_ref, g_ref, wnn_ref, wt_ref, kaux_ref,
                   inner_ref, bc_ref, gates_ref, kp_ref, qt_ref, vt_ref, kn2_ref):
    u = _rms(h_ref[...], g_ref[...]).astype(BF16)

    def nn(lo, hi):
        return jnp.dot(u, wnn_ref[:, lo:hi], preferred_element_type=F32)

    inner_ref[...] = nn(OFF_CC, OFF_K) * nn(0, OFF_BC)
    bc_ref[...] = nn(OFF_BC, OFF_CC)
    kf = nn(OFF_K, OFF_G)
    kp_ref[...] = (kf + kaux_ref[...]).astype(BF16)
    for hm in range(2 * N_HEADS):
        sq = jnp.sum(jnp.square(kf[:, hm * K_LANES:(hm + 1) * K_LANES]), axis=1, keepdims=True)
        kn2_ref[0, hm:hm + 1, :] = jnp.broadcast_to(
            jnp.max(sq, axis=0, keepdims=True), (1, V7X_LANES))
    gates_ref[:, :D_MODEL] = nn(OFF_G, OFF_G + D_MODEL)
    gates_ref[:, D_MODEL:] = nn(OFF_G + D_MODEL, W_NN_COLS)

    qv = lax.dot_general(wt_ref[...], u, (((1,), (1,)), ((), ())),
                         preferred_element_type=F32)
    qt_ref[0] = (qv[:ATTN_WIDTH] * Q_SCALE).astype(BF16)
    tm = u.shape[0]
    ones_rows = jnp.where(
        lax.broadcasted_iota(jnp.int32, (V_ROWS - V_DIM, tm), 0) == 0, 1.0, 0.0).astype(BF16)
    for h in range(N_HEADS):
        lo = ATTN_WIDTH + h * V_DIM
        vt_ref[0, h, 0, :V_DIM, :] = qv[lo:lo + V_DIM].astype(BF16)
        vt_ref[0, h, 0, V_DIM:, :] = ones_rows


def _inproj(h, g, wnn, wt, kaux, batch, seq):
    n_tok = h.shape[0]
    tm = ROW_TILE
    tpb = seq // tm
    grid = (n_tok // tm,)
    row = lambda i: (i, 0)
    const = lambda i: (0, 0)
    out_shape = (
        jax.ShapeDtypeStruct((n_tok, CONV_WIDTH), F32),
        jax.ShapeDtypeStruct((n_tok, CONV_WIDTH), F32),
        jax.ShapeDtypeStruct((n_tok, 2 * D_MODEL), F32),
        jax.ShapeDtypeStruct((n_tok, 2 * N_HEADS * K_LANES), BF16),
        jax.ShapeDtypeStruct((batch, ATTN_WIDTH, seq), BF16),
        jax.ShapeDtypeStruct((batch, N_HEADS, tpb, V_ROWS, tm), BF16),
        jax.ShapeDtypeStruct((n_tok // tm, 2 * N_HEADS, V7X_LANES), F32),
    )
    in_specs = [
        pl.BlockSpec((tm, D_MODEL), row),
        pl.BlockSpec((1, D_MODEL), const),
        pl.BlockSpec((D_MODEL, W_NN_COLS), const),
        pl.BlockSpec((2 * ATTN_WIDTH, D_MODEL), const),
        pl.BlockSpec((tm, 2 * N_HEADS * K_LANES), const),
    ]
    out_specs = (
        pl.BlockSpec((tm, CONV_WIDTH), row),
        pl.BlockSpec((tm, CONV_WIDTH), row),
        pl.BlockSpec((tm, 2 * D_MODEL), row),
        pl.BlockSpec((tm, 2 * N_HEADS * K_LANES), row),
        pl.BlockSpec((1, ATTN_WIDTH, tm), lambda i: (i // tpb, 0, i % tpb)),
        pl.BlockSpec((1, N_HEADS, 1, V_ROWS, tm), lambda i: (i // tpb, 0, i % tpb, 0, 0)),
        pl.BlockSpec((1, 2 * N_HEADS, V7X_LANES), lambda i: (i, 0, 0)),
    )
    blocks = (_nbytes((tm, D_MODEL), F32) + _nbytes((D_MODEL, W_NN_COLS), BF16)
              + _nbytes((2 * ATTN_WIDTH, D_MODEL), BF16) + _nbytes((tm, 1024), F32)
              + 2 * _nbytes((tm, CONV_WIDTH), F32) + _nbytes((tm, 2 * D_MODEL), F32)
              + _nbytes((tm, 1024), BF16) + _nbytes((ATTN_WIDTH, tm), BF16)
              + _nbytes((N_HEADS, V_ROWS, tm), BF16))
    temps = 4 * _nbytes((tm, D_MODEL), F32) + _nbytes((2 * ATTN_WIDTH, tm), F32)
    return pl.pallas_call(
        _inproj_kernel, grid=grid, in_specs=in_specs, out_specs=out_specs, out_shape=out_shape,
        compiler_params=pltpu.CompilerParams(
            dimension_semantics=("arbitrary",), vmem_limit_bytes=_vmem_limit(blocks, temps)),
        name="inproj",
    )(h, g, wnn, wt, kaux)


def _attn_kernel(cs_ref, lq_ref, g_ref, dist_ref, kn2_ref, qt_ref, kp_ref, vt_ref, o_ref,
                 qv_ref, sa_ref, sb_ref, acc_ref, *, lam_init):
    h = pl.program_id(1)
    qi = pl.program_id(2)
    tq, tk = Q_TILE, K_TILE
    n_kb = kp_ref.shape[1] // tk
    c = cs_ref[0, h]
    q0 = qi * tq
    kd = q0 // tk

    aux_rows = lax.broadcasted_iota(jnp.int32, (K_LANES - HEAD_DIM, tq), 0) < N_AUX
    for v, sign in enumerate((1.0, -1.0, 0.0)):
        for m in range(2):
            qv_ref[v, m, :HEAD_DIM, :] = qt_ref[0, m * HEAD_DIM:(m + 1) * HEAD_DIM, :]
            qv_ref[v, m, HEAD_DIM:, :] = jnp.where(aux_rows, sign, 0.0).astype(BF16)

    ci = c * (q0 + lax.broadcasted_iota(jnp.int32, (1, tq), 1)).astype(F32)
    acc_ref[...] = jnp.zeros_like(acc_ref)

    def scores(kb, variant, s_ref, bias):
        kblk = kp_ref[0, pl.ds(pl.multiple_of(kb * tk, tk), tk), :]
        cms = []
        for m in range(2):
            st = jnp.dot(kblk[:, m * K_LANES:(m + 1) * K_LANES], qv_ref[variant, m],
                         preferred_element_type=F32)
            if bias is not None:
                st = st + bias
            s_ref[m] = st
            cms.append(jnp.max(st, axis=0, keepdims=True))
        return tuple(cms)

    def fold(kb, s_ref, cms, off, ms):
        vblk = vt_ref[0, 0, kb]
        new_ms = []
        for m in range(2):
            m_new = jnp.maximum(ms[m], cms[m] + off)
            alpha = jnp.exp2(ms[m] - m_new)
            p = jnp.exp2(s_ref[m] - (m_new - off)).astype(BF16)
            acc_ref[m] = acc_ref[m] * alpha + jnp.dot(vblk, p, preferred_element_type=F32)
            new_ms.append(m_new)
        return tuple(new_ms)

    cms = scores(kd, 2, sa_ref, dist_ref[qi % (tk // tq)] * (-c))
    m0 = jnp.full((1, tq), M_INIT, F32)

    need = None
    for m in range(2):
        qf = qt_ref[0, m * HEAD_DIM:(m + 1) * HEAD_DIM, :].astype(F32)
        qn = jnp.sqrt(jnp.max(jnp.sum(qf * qf, axis=0, keepdims=True), axis=1, keepdims=True))
        kn = jnp.sqrt(jnp.max(jnp.max(kn2_ref[0, 0, m], axis=0, keepdims=True), axis=1, keepdims=True))
        need_m = qn * kn * NORM_SLACK - jnp.min(cms[m], axis=1, keepdims=True)
        need = need_m if need is None else jnp.maximum(need, need_m)
    radius = (need + SKIP_MARGIN) * cs_ref[1, h] + 1.0
    q0f, kdf = q0.astype(F32), kd.astype(F32)
    lo = jnp.clip(jnp.floor((q0f - (tk - 1) - radius) / tk), 0.0, kdf).astype(jnp.int32)[0, 0]
    hi = jnp.clip(jnp.ceil((radius + q0f + (tq - 1)) / tk), kdf, n_kb - 1.0).astype(jnp.int32)[0, 0]
    n_steps = hi - lo

    def step(i, s_next, s_prev, carry):
        ms, cms_prev, kb_prev, off_prev = carry
        kb = lo + i
        kb = kb + (kb >= kd).astype(jnp.int32)
        left = kb < kd
        cms_next = scores(kb, jnp.where(left, 0, 1), s_next, None)
        ms = fold(kb_prev, s_prev, cms_prev, off_prev, ms)
        off = jnp.where(left, 1.0, -1.0) * (c * (kb * tk).astype(F32) - ci)
        return ms, cms_next, kb, off

    bufs = (sa_ref, sb_ref)

    def steps(first, n, carry):
        for u in range(n):
            carry = step(first + u, bufs[(u + 1) % 2], bufs[u % 2], carry)
        return carry

    n_main = n_steps // STEPS_PER_ITER
    carry = lax.fori_loop(
        0, n_main, lambda j, carry: steps(j * STEPS_PER_ITER, STEPS_PER_ITER, carry),
        ((m0, m0), cms, kd, jnp.zeros((1, tq), F32)))
    done = n_main * STEPS_PER_ITER
    n_pairs = (n_steps - done) // 2
    carry = lax.fori_loop(0, n_pairs, lambda j, carry: steps(done + 2 * j, 2, carry), carry)
    done = done + 2 * n_pairs

    def tail_odd(carry):
        ms, cms, kb_last, off_last = steps(done, 1, carry)
        fold(kb_last, bufs[1], cms, off_last, ms)

    def tail_even(carry):
        ms, cms, kb_last, off_last = carry
        fold(kb_last, bufs[0], cms, off_last, ms)

    lax.cond(n_steps - done == 1, tail_odd, tail_even, carry)

    lq = lq_ref[...]
    lam = (jnp.exp(jnp.sum(lq[0:1] * lq[1:2], axis=-1, keepdims=True))
           - jnp.exp(jnp.sum(lq[2:3] * lq[3:4], axis=-1, keepdims=True)) + lam_init)
    o1 = acc_ref[0, :V_DIM, :] / acc_ref[0, V_DIM:V_DIM + 1, :]
    o2 = acc_ref[1, :V_DIM, :] / acc_ref[1, V_DIM:V_DIM + 1, :]
    ot = o1 - lam * o2
    y = ot * lax.rsqrt(jnp.mean(ot * ot, axis=0, keepdims=True) + EPS) * g_ref[...]
    y = y * (1.0 - lam_init)
    o_ref[0] = y.T.astype(BF16)


def _attention(cs, lq, g_col, dist, kn2, qt, kp, vt, lam_init, batch, seq):
    tq, tk = Q_TILE, K_TILE
    grid = (batch, N_HEADS, seq // tq)
    kn2 = kn2.reshape(batch, seq // tk, N_HEADS, 2, V7X_LANES).transpose(0, 2, 3, 1, 4)
    in_specs = [
        pl.BlockSpec(memory_space=pltpu.SMEM),
        pl.BlockSpec((4, HEAD_DIM), lambda b, h, q: (0, 0)),
        pl.BlockSpec((V_DIM, 1), lambda b, h, q: (0, 0)),
        pl.BlockSpec((tk // tq, tk, tq), lambda b, h, q: (0, 0, 0)),
        pl.BlockSpec((1, 1, 2, seq // tk, V7X_LANES), lambda b, h, q: (b, h, 0, 0, 0)),
        pl.BlockSpec((1, 2 * HEAD_DIM, tq), lambda b, h, q: (b, h, q)),
        pl.BlockSpec((1, seq, 2 * K_LANES), lambda b, h, q: (b, 0, h)),
        pl.BlockSpec((1, 1, seq // tk, V_ROWS, tk), lambda b, h, q: (b, h, 0, 0, 0)),
    ]
    out_specs = pl.BlockSpec((1, tq, V_DIM), lambda b, h, q: (b, q, h))
    blocks = (_nbytes((seq, 2 * K_LANES), BF16) + _nbytes((seq // tk, V_ROWS, tk), BF16)
              + _nbytes((2 * HEAD_DIM, tq), BF16) + _nbytes((tq, V_DIM), BF16)
              + _nbytes((V_DIM, V7X_LANES), F32) + _nbytes((V7X_SUBLANES, V7X_LANES), F32)
              + _nbytes((tk // tq, tk, tq), F32))
    temps = 2 * _nbytes((V_ROWS, tq), F32) + 12 * _nbytes((tk, tq), F32)
    return pl.pallas_call(
        functools.partial(_attn_kernel, lam_init=lam_init),
        grid=grid, in_specs=in_specs, out_specs=out_specs,
        out_shape=jax.ShapeDtypeStruct((batch, seq, ATTN_WIDTH), BF16),
        scratch_shapes=[pltpu.VMEM((3, 2, K_LANES, tq), BF16),
                        pltpu.VMEM((2, tk, tq), F32),
                        pltpu.VMEM((2, tk, tq), F32),
                        pltpu.VMEM((2, V_ROWS, tq), F32)],
        compiler_params=pltpu.CompilerParams(
            dimension_semantics=("arbitrary", "arbitrary", "arbitrary"),
            vmem_limit_bytes=_vmem_limit(blocks, temps)),
        name="attn",
    )(cs, lq, g_col, dist, kn2, qt, kp.reshape(batch, seq, -1), vt)


def _mix_kernel(inner_ref, prev_ref, next_ref, bc_ref, o_ref, gates_ref, h_ref,
                cw_ref, bg_ref, wco_ref, wao_ref, wo_ref, g_ref, out_ref, *, tiles_per_seq):
    t = pl.program_id(0) % tiles_per_seq
    x = inner_ref[...]
    tm = x.shape[0]
    prev_row = jnp.where(t == 0, 0.0, prev_ref[V7X_SUBLANES - 1:V7X_SUBLANES, :])
    next_row = jnp.where(t == tiles_per_seq - 1, 0.0, next_ref[0:1, :])
    row = lax.broadcasted_iota(jnp.int32, x.shape, 0)
    up = jnp.where(row == 0, prev_row, pltpu.roll(x, 1, 0))
    dn = jnp.where(row == tm - 1, next_row, pltpu.roll(x, tm - 1, 0))
    conv = up * cw_ref[0:1, :] + x * cw_ref[1:2, :] + dn * cw_ref[2:3, :]
    z = (bc_ref[...] * conv).astype(BF16)
    y_conv = jnp.dot(z, wco_ref[...], preferred_element_type=F32)
    y_attn = jnp.dot(o_ref[...], wao_ref[...], preferred_element_type=F32)
    gate = jax.nn.sigmoid(gates_ref[...] + bg_ref[...])
    merged = (gate[:, :D_MODEL] * y_conv + gate[:, D_MODEL:] * y_attn).astype(BF16)
    m = jnp.dot(merged, wo_ref[...], preferred_element_type=F32)
    out_ref[...] = h_ref[...] + _rms(m, g_ref[...])


def _mix(inner, bc, o, gates, h, cw, bg, wco, wao, wo, g, seq):
    n_tok = h.shape[0]
    tm = ROW_TILE
    hb = tm // V7X_SUBLANES
    n_hb = n_tok // V7X_SUBLANES
    grid = (n_tok // tm,)
    row = lambda i: (i, 0)
    const = lambda i: (0, 0)
    in_specs = [
        pl.BlockSpec((tm, CONV_WIDTH), row),
        pl.BlockSpec((V7X_SUBLANES, CONV_WIDTH), lambda i: (jnp.maximum(i * hb - 1, 0), 0)),
        pl.BlockSpec((V7X_SUBLANES, CONV_WIDTH), lambda i: (jnp.minimum((i + 1) * hb, n_hb - 1), 0)),
        pl.BlockSpec((tm, CONV_WIDTH), row),
        pl.BlockSpec((tm, ATTN_WIDTH), row),
        pl.BlockSpec((tm, 2 * D_MODEL), row),
        pl.BlockSpec((tm, D_MODEL), row),
        pl.BlockSpec((CONV_K, CONV_WIDTH), const),
        pl.BlockSpec((1, 2 * D_MODEL), const),
        pl.BlockSpec((CONV_WIDTH, D_MODEL), const),
        pl.BlockSpec((ATTN_WIDTH, D_MODEL), const),
        pl.BlockSpec((D_MODEL, D_MODEL), const),
        pl.BlockSpec((1, D_MODEL), const),
    ]
    blocks = (2 * _nbytes((tm, CONV_WIDTH), F32) + _nbytes((tm, ATTN_WIDTH), BF16)
              + _nbytes((tm, 2 * D_MODEL), F32) + 2 * _nbytes((tm, D_MODEL), F32)
              + _nbytes((2 * CONV_WIDTH + D_MODEL, D_MODEL), BF16) + 8 * _nbytes((8, 2 * D_MODEL), F32))
    temps = 8 * _nbytes((tm, D_MODEL), F32)
    return pl.pallas_call(
        functools.partial(_mix_kernel, tiles_per_seq=seq // tm),
        grid=grid, in_specs=in_specs, out_specs=pl.BlockSpec((tm, D_MODEL), row),
        out_shape=jax.ShapeDtypeStruct((n_tok, D_MODEL), F32),
        compiler_params=pltpu.CompilerParams(
            dimension_semantics=("arbitrary",), vmem_limit_bytes=_vmem_limit(blocks, temps)),
        name="mix",
    )(inner, inner, inner, bc, o, gates, h, cw, bg, wco, wao, wo, g)


FF_CHUNK = 1024


def _ffn_kernel(h_ref, g_in_ref, w1_ref, w2_ref, g_out_ref, out_ref):
    x = h_ref[...]
    u = _rms(x, g_in_ref[...]).astype(BF16)
    acc = jnp.zeros(x.shape, F32)
    for lo in range(0, D_FF, FF_CHUNK):
        hid = jnp.dot(u, w1_ref[:, lo:lo + FF_CHUNK], preferred_element_type=F32)
        hid = jnp.square(jnp.maximum(hid, 0.0)).astype(BF16)
        acc = acc + jnp.dot(hid, w2_ref[lo:lo + FF_CHUNK, :], preferred_element_type=F32)
    out_ref[...] = x + _rms(acc, g_out_ref[...])


def _ffn(h, g_in, w1, w2, g_out):
    n_tok = h.shape[0]
    tm = ROW_TILE
    row = lambda i: (i, 0)
    const = lambda i: (0, 0)
    in_specs = [
        pl.BlockSpec((tm, D_MODEL), row),
        pl.BlockSpec((1, D_MODEL), const),
        pl.BlockSpec((D_MODEL, D_FF), const),
        pl.BlockSpec((D_FF, D_MODEL), const),
        pl.BlockSpec((1, D_MODEL), const),
    ]
    blocks = 2 * _nbytes((tm, D_MODEL), F32) + 2 * _nbytes((D_MODEL, D_FF), BF16)
    temps = 6 * _nbytes((tm, D_MODEL), F32)
    return pl.pallas_call(
        _ffn_kernel, grid=(n_tok // tm,), in_specs=in_specs,
        out_specs=pl.BlockSpec((tm, D_MODEL), row),
        out_shape=jax.ShapeDtypeStruct((n_tok, D_MODEL), F32),
        compiler_params=pltpu.CompilerParams(
            dimension_semantics=("arbitrary",), vmem_limit_bytes=_vmem_limit(blocks, temps)),
        name="ffn",
    )(h, g_in, w1, w2, g_out)


def _alibi_consts():
    slopes = jnp.exp2(-8.0 * jnp.arange(1, N_HEADS + 1, dtype=F32) / N_HEADS)
    cs = slopes * LOG2E
    x = cs[None, :] * jnp.arange(K_TILE, dtype=F32)[:, None]
    terms, rest = [], x
    for _ in range(N_AUX):
        t = rest.astype(BF16).astype(F32)
        terms.append(t)
        rest = rest - t
    aux = jnp.stack(terms, axis=-1)
    lanes = jnp.zeros((K_TILE, N_HEADS, 2, K_LANES), F32)
    lanes = lanes.at[:, :, :, HEAD_DIM:HEAD_DIM + N_AUX].set(aux[:, :, None, :])
    n_pos = K_TILE // Q_TILE
    qpos = jnp.arange(Q_TILE)[None, None, :] + Q_TILE * jnp.arange(n_pos)[:, None, None]
    dist = jnp.abs(qpos - jnp.arange(K_TILE)[None, :, None]).astype(F32)
    return jnp.stack([cs, 1.0 / cs]), lanes.reshape(K_TILE, 2 * N_HEADS * K_LANES), dist


def _prep_in_weights(w_in):
    cw, aw = CONV_WIDTH, ATTN_WIDTH
    w_conv = w_in[:, :3 * cw]
    w_q = w_in[:, 3 * cw:3 * cw + aw]
    w_k = w_in[:, 3 * cw + aw:3 * cw + 2 * aw]
    w_v = w_in[:, 3 * cw + 2 * aw:3 * cw + 3 * aw]
    w_g = w_in[:, 3 * cw + 3 * aw:]
    w_kp = jnp.pad(w_k.reshape(D_MODEL, 2 * N_HEADS, HEAD_DIM),
                   ((0, 0), (0, 0), (0, K_LANES - HEAD_DIM))).reshape(D_MODEL, -1)
    wnn = jnp.concatenate([w_conv, w_kp, w_g], axis=1).astype(BF16)
    wt = jnp.concatenate([w_q, w_v], axis=1).T.astype(BF16)
    return wnn, wt


def kernel(x, w_in, b_gate, conv_w, w_conv_out, lambda_qk, subln_g, w_attn_out, w_o, norm_g,
           w_ff1, w_ff2):
    batch, seq, _ = x.shape
    depth = w_in.shape[0]
    assert seq % ROW_TILE == 0 and seq % Q_TILE == 0 and K_TILE % Q_TILE == 0
    cs, kaux, dist = _alibi_consts()
    h = x.reshape(batch * seq, D_MODEL)
    for l in range(depth):
        lam_init = 0.8 - 0.6 * math.exp(-0.3 * l)
        wnn, wt = _prep_in_weights(w_in[l])
        inner, bc, gates, kp, qt, vt, kn2 = _inproj(h, norm_g[l, 0:1], wnn, wt, kaux, batch, seq)
        o = _attention(cs, lambda_qk[l], subln_g[l].reshape(V_DIM, 1), dist, kn2, qt, kp, vt,
                       lam_init, batch, seq)
        h = _mix(inner, bc, o.reshape(batch * seq, ATTN_WIDTH), gates, h, conv_w[l],
                 b_gate[l].reshape(1, -1), w_conv_out[l].astype(BF16), w_attn_out[l].astype(BF16),
                 w_o[l].astype(BF16), norm_g[l, 1:2], seq)
        h = _ffn(h, norm_g[l, 2:3], w_ff1[l].astype(BF16), w_ff2[l].astype(BF16), norm_g[l, 3:4])
    return h.reshape(batch, seq, D_MODEL)
```

```python
import functools
import math

import jax
import jax.numpy as jnp
from jax import lax
from jax.experimental import pallas as pl
from jax.experimental.pallas import tpu as pltpu

D_MODEL = 1024
CONV_WIDTH = 512
CONV_K = 3
N_HEADS = 4
HEAD_DIM = 64
V_DIM = 128
ATTN_WIDTH = N_HEADS * V_DIM
D_FF = 4 * D_MODEL
EPS = 1e-6

F32 = jnp.float32
BF16 = jnp.bfloat16

V7X_LANES = 128
V7X_SUBLANES = 8
V7X_BF16_ROWS_PER_VREG = 16
V7X_MXU_DIM = 256
V7X_VMEM_BYTES = 64 * 1024 * 1024

ROW_TILE = 512
Q_TILE = 2 * V7X_MXU_DIM
K_TILE = ROW_TILE
K_LANES = 2 * HEAD_DIM
N_AUX = 3
V_ROWS = V_DIM + V7X_BF16_ROWS_PER_VREG
LOG2E = math.log2(math.e)
Q_SCALE = HEAD_DIM ** -0.5 * LOG2E
M_INIT = -1e30
STEPS_PER_ITER = 6
SKIP_MARGIN = 152.0
NORM_SLACK = 1.01

W_NN_COLS = 3 * CONV_WIDTH + 2 * N_HEADS * K_LANES
OFF_BC, OFF_CC, OFF_K = 512, 1024, 1536


def _vmem_limit(block_bytes, temp_bytes):
    need = 2 * block_bytes + temp_bytes
    assert need < V7X_VMEM_BYTES, need
    return int(need)


def _nbytes(shape, dtype):
    return math.prod(shape) * jnp.dtype(dtype).itemsize


def _rms(x, g):
    return x * lax.rsqrt(jnp.mean(x * x, axis=-1, keepdims=True) + EPS) * g


def _inproj_kernel(h_ref, g_ref, wnn_ref, wt_ref, kaux_ref,
                   inner_ref, bc_ref, kp_ref, qt_ref, vt_ref, kn2_ref):
    u = _rms(h_ref[...], g_ref[...]).astype(BF16)

    def nn(lo, hi):
        return jnp.dot(u, wnn_ref[:, lo:hi], preferred_element_type=F32)

    inner_ref[...] = nn(OFF_CC, OFF_K) * nn(0, OFF_BC)
    bc_ref[...] = nn(OFF_BC, OFF_CC)
    kf = nn(OFF_K, W_NN_COLS)
    kp_ref[...] = (kf + kaux_ref[...]).astype(BF16)
    for hm in range(2 * N_HEADS):
        sq = jnp.sum(jnp.square(kf[:, hm * K_LANES:(hm + 1) * K_LANES]), axis=1, keepdims=True)
        kn2_ref[0, hm:hm + 1, :] = jnp.broadcast_to(
            jnp.max(sq, axis=0, keepdims=True), (1, V7X_LANES))

    qv = lax.dot_general(wt_ref[...], u, (((1,), (1,)), ((), ())),
                         preferred_element_type=F32)
    qt_ref[0] = (qv[:ATTN_WIDTH] * Q_SCALE).astype(BF16)
    tm = u.shape[0]
    ones_rows = jnp.where(
        lax.broadcasted_iota(jnp.int32, (V_ROWS - V_DIM, tm), 0) == 0, 1.0, 0.0).astype(BF16)
    for h in range(N_HEADS):
        lo = ATTN_WIDTH + h * V_DIM
        vt_ref[0, h, 0, :V_DIM, :] = qv[lo:lo + V_DIM].astype(BF16)
        vt_ref[0, h, 0, V_DIM:, :] = ones_rows


def _inproj(h, g, wnn, wt, kaux, batch, seq):
    n_tok = h.shape[0]
    tm = ROW_TILE
    tpb = seq // tm
    grid = (n_tok // tm,)
    row = lambda i: (i, 0)
    const = lambda i: (0, 0)
    out_shape = (
        jax.ShapeDtypeStruct((n_tok, CONV_WIDTH), F32),
        jax.ShapeDtypeStruct((n_tok, CONV_WIDTH), F32),
        jax.ShapeDtypeStruct((n_tok, 2 * N_HEADS * K_LANES), BF16),
        jax.ShapeDtypeStruct((batch, ATTN_WIDTH, seq), BF16),
        jax.ShapeDtypeStruct((batch, N_HEADS, tpb, V_ROWS, tm), BF16),
        jax.ShapeDtypeStruct((n_tok // tm, 2 * N_HEADS, V7X_LANES), F32),
    )
    in_specs = [
        pl.BlockSpec((tm, D_MODEL), row),
        pl.BlockSpec((1, D_MODEL), const),
        pl.BlockSpec((D_MODEL, W_NN_COLS), const),
        pl.BlockSpec((2 * ATTN_WIDTH, D_MODEL), const),
        pl.BlockSpec((tm, 2 * N_HEADS * K_LANES), const),
    ]
    out_specs = (
        pl.BlockSpec((tm, CONV_WIDTH), row),
        pl.BlockSpec((tm, CONV_WIDTH), row),
        pl.BlockSpec((tm, 2 * N_HEADS * K_LANES), row),
        pl.BlockSpec((1, ATTN_WIDTH, tm), lambda i: (i // tpb, 0, i % tpb)),
        pl.BlockSpec((1, N_HEADS, 1, V_ROWS, tm), lambda i: (i // tpb, 0, i % tpb, 0, 0)),
        pl.BlockSpec((1, 2 * N_HEADS, V7X_LANES), lambda i: (i, 0, 0)),
    )
    blocks = (_nbytes((tm, D_MODEL), F32) + _nbytes((D_MODEL, W_NN_COLS), BF16)
              + _nbytes((2 * ATTN_WIDTH, D_MODEL), BF16) + _nbytes((tm, 1024), F32)
              + 2 * _nbytes((tm, CONV_WIDTH), F32)
              + _nbytes((tm, 1024), BF16) + _nbytes((ATTN_WIDTH, tm), BF16)
              + _nbytes((N_HEADS, V_ROWS, tm), BF16))
    temps = 4 * _nbytes((tm, D_MODEL), F32) + _nbytes((2 * ATTN_WIDTH, tm), F32)
    return pl.pallas_call(
        _inproj_kernel, grid=grid, in_specs=in_specs, out_specs=out_specs, out_shape=out_shape,
        compiler_params=pltpu.CompilerParams(
            dimension_semantics=("arbitrary",), vmem_limit_bytes=_vmem_limit(blocks, temps)),
        name="inproj",
    )(h, g, wnn, wt, kaux)


def _attn_kernel(cs_ref, lq_ref, g_ref, dist_ref, kn2_ref, qt_ref, kp_ref, vt_ref, o_ref,
                 qv_ref, sa_ref, sb_ref, acc_ref, *, lam_init):
    h = pl.program_id(1)
    qi = pl.program_id(2)
    tq, tk = Q_TILE, K_TILE
    n_kb = kp_ref.shape[1] // tk
    c = cs_ref[0, h]
    q0 = qi * tq
    kd = q0 // tk

    aux_rows = lax.broadcasted_iota(jnp.int32, (K_LANES - HEAD_DIM, tq), 0) < N_AUX
    for v, sign in enumerate((1.0, -1.0, 0.0)):
        for m in range(2):
            qv_ref[v, m, :HEAD_DIM, :] = qt_ref[0, m * HEAD_DIM:(m + 1) * HEAD_DIM, :]
            qv_ref[v, m, HEAD_DIM:, :] = jnp.where(aux_rows, sign, 0.0).astype(BF16)

    ci = c * (q0 + lax.broadcasted_iota(jnp.int32, (1, tq), 1)).astype(F32)
    acc_ref[...] = jnp.zeros_like(acc_ref)

    def scores(kb, variant, s_ref, bias):
        kblk = kp_ref[0, pl.ds(pl.multiple_of(kb * tk, tk), tk), :]
        cms = []
        for m in range(2):
            st = jnp.dot(kblk[:, m * K_LANES:(m + 1) * K_LANES], qv_ref[variant, m],
                         preferred_element_type=F32)
            if bias is not None:
                st = st + bias
            s_ref[m] = st
            cms.append(jnp.max(st, axis=0, keepdims=True))
        return tuple(cms)

    def fold(kb, s_ref, cms, off, ms):
        vblk = vt_ref[0, 0, kb]
        new_ms = []
        for m in range(2):
            m_new = jnp.maximum(ms[m], cms[m] + off)
            alpha = jnp.exp2(ms[m] - m_new)
            p = jnp.exp2(s_ref[m] - (m_new - off)).astype(BF16)
            acc_ref[m] = acc_ref[m] * alpha + jnp.dot(vblk, p, preferred_element_type=F32)
            new_ms.append(m_new)
        return tuple(new_ms)

    cms = scores(kd, 2, sa_ref, dist_ref[qi % (tk // tq)] * (-c))
    m0 = jnp.full((1, tq), M_INIT, F32)

    need = None
    for m in range(2):
        qf = qt_ref[0, m * HEAD_DIM:(m + 1) * HEAD_DIM, :].astype(F32)
        qn = jnp.sqrt(jnp.max(jnp.sum(qf * qf, axis=0, keepdims=True), axis=1, keepdims=True))
        kn = jnp.sqrt(jnp.max(jnp.max(kn2_ref[0, 0, m], axis=0, keepdims=True), axis=1, keepdims=True))
        need_m = qn * kn * NORM_SLACK - jnp.min(cms[m], axis=1, keepdims=True)
        need = need_m if need is None else jnp.maximum(need, need_m)
    radius = (need + SKIP_MARGIN) * cs_ref[1, h] + 1.0
    q0f, kdf = q0.astype(F32), kd.astype(F32)
    lo = jnp.clip(jnp.floor((q0f - (tk - 1) - radius) / tk), 0.0, kdf).astype(jnp.int32)[0, 0]
    hi = jnp.clip(jnp.ceil((radius + q0f + (tq - 1)) / tk), kdf, n_kb - 1.0).astype(jnp.int32)[0, 0]
    lo = jnp.clip(lo, 0, kd)
    hi = jnp.clip(hi, kd, n_kb - 1)
    n_steps = hi - lo

    def step(i, s_next, s_prev, carry):
        ms, cms_prev, kb_prev, off_prev = carry
        kb = lo + i
        kb = kb + (kb >= kd).astype(jnp.int32)
        left = kb < kd
        cms_next = scores(kb, jnp.where(left, 0, 1), s_next, None)
        ms = fold(kb_prev, s_prev, cms_prev, off_prev, ms)
        off = jnp.where(left, 1.0, -1.0) * (c * (kb * tk).astype(F32) - ci)
        return ms, cms_next, kb, off

    bufs = (sa_ref, sb_ref)

    def steps(first, n, carry):
        for u in range(n):
            carry = step(first + u, bufs[(u + 1) % 2], bufs[u % 2], carry)
        return carry

    n_main = n_steps // STEPS_PER_ITER
    carry = lax.fori_loop(
        0, n_main, lambda j, carry: steps(j * STEPS_PER_ITER, STEPS_PER_ITER, carry),
        ((m0, m0), cms, kd, jnp.zeros((1, tq), F32)))
    done = n_main * STEPS_PER_ITER
    n_pairs = (n_steps - done) // 2
    carry = lax.fori_loop(0, n_pairs, lambda j, carry: steps(done + 2 * j, 2, carry), carry)
    done = done + 2 * n_pairs

    def tail_odd(carry):
        ms, cms, kb_last, off_last = steps(done, 1, carry)
        fold(kb_last, bufs[1], cms, off_last, ms)

    def tail_even(carry):
        ms, cms, kb_last, off_last = carry
        fold(kb_last, bufs[0], cms, off_last, ms)

    lax.cond(n_steps - done == 1, tail_odd, tail_even, carry)

    lq = lq_ref[...]
    lam = (jnp.exp(jnp.sum(lq[0:1] * lq[1:2], axis=-1, keepdims=True))
           - jnp.exp(jnp.sum(lq[2:3] * lq[3:4], axis=-1, keepdims=True)) + lam_init)
    o1 = acc_ref[0, :V_DIM, :] / acc_ref[0, V_DIM:V_DIM + 1, :]
    o2 = acc_ref[1, :V_DIM, :] / acc_ref[1, V_DIM:V_DIM + 1, :]
    ot = o1 - lam * o2
    y = ot * lax.rsqrt(jnp.mean(ot * ot, axis=0, keepdims=True) + EPS) * g_ref[...]
    y = y * (1.0 - lam_init)
    o_ref[0] = y.T.astype(BF16)


def _attention(cs, lq, g_col, dist, kn2, qt, kp, vt, lam_init, batch, seq):
    tq, tk = Q_TILE, K_TILE
    grid = (batch, N_HEADS, seq // tq)
    kn2 = kn2.reshape(batch, seq // tk, N_HEADS, 2, V7X_LANES).transpose(0, 2, 3, 1, 4)
    in_specs = [
        pl.BlockSpec(memory_space=pltpu.SMEM),
        pl.BlockSpec((4, HEAD_DIM), lambda b, h, q: (0, 0)),
        pl.BlockSpec((V_DIM, 1), lambda b, h, q: (0, 0)),
        pl.BlockSpec((tk // tq, tk, tq), lambda b, h, q: (0, 0, 0)),
        pl.BlockSpec((1, 1, 2, seq // tk, V7X_LANES), lambda b, h, q: (b, h, 0, 0, 0)),
        pl.BlockSpec((1, 2 * HEAD_DIM, tq), lambda b, h, q: (b, h, q)),
        pl.BlockSpec((1, seq, 2 * K_LANES), lambda b, h, q: (b, 0, h)),
        pl.BlockSpec((1, 1, seq // tk, V_ROWS, tk), lambda b, h, q: (b, h, 0, 0, 0)),
    ]
    out_specs = pl.BlockSpec((1, tq, V_DIM), lambda b, h, q: (b, q, h))
    blocks = (_nbytes((seq, 2 * K_LANES), BF16) + _nbytes((seq // tk, V_ROWS, tk), BF16)
              + _nbytes((2 * HEAD_DIM, tq), BF16) + _nbytes((tq, V_DIM), BF16)
              + _nbytes((V_DIM, V7X_LANES), F32) + _nbytes((V7X_SUBLANES, V7X_LANES), F32)
              + _nbytes((tk // tq, tk, tq), F32))
    temps = 2 * _nbytes((V_ROWS, tq), F32) + 12 * _nbytes((tk, tq), F32)
    return pl.pallas_call(
        functools.partial(_attn_kernel, lam_init=lam_init),
        grid=grid, in_specs=in_specs, out_specs=out_specs,
        out_shape=jax.ShapeDtypeStruct((batch, seq, ATTN_WIDTH), BF16),
        scratch_shapes=[pltpu.VMEM((3, 2, K_LANES, tq), BF16),
                        pltpu.VMEM((2, tk, tq), F32),
                        pltpu.VMEM((2, tk, tq), F32),
                        pltpu.VMEM((2, V_ROWS, tq), F32)],
        compiler_params=pltpu.CompilerParams(
            dimension_semantics=("arbitrary", "arbitrary", "arbitrary"),
            vmem_limit_bytes=_vmem_limit(blocks, temps)),
        name="attn",
    )(cs, lq, g_col, dist, kn2, qt, kp.reshape(batch, seq, -1), vt)


def _mix_kernel(inner_ref, prev_ref, next_ref, bc_ref, o_ref, h_ref, g_in_ref, wg_ref,
                cw_ref, bg_ref, wco_ref, wao_ref, wo_ref, g_ref, out_ref, *, tiles_per_seq):
    t = pl.program_id(0) % tiles_per_seq
    h = h_ref[...]
    u = _rms(h, g_in_ref[...]).astype(BF16)
    gate = jax.nn.sigmoid(jnp.dot(u, wg_ref[...], preferred_element_type=F32) + bg_ref[...])
    x = inner_ref[...]
    tm = x.shape[0]
    prev_row = jnp.where(t == 0, 0.0, prev_ref[V7X_SUBLANES - 1:V7X_SUBLANES, :])
    next_row = jnp.where(t == tiles_per_seq - 1, 0.0, next_ref[0:1, :])
    row = lax.broadcasted_iota(jnp.int32, x.shape, 0)
    up = jnp.where(row == 0, prev_row, pltpu.roll(x, 1, 0))
    dn = jnp.where(row == tm - 1, next_row, pltpu.roll(x, tm - 1, 0))
    conv = up * cw_ref[0:1, :] + x * cw_ref[1:2, :] + dn * cw_ref[2:3, :]
    z = (bc_ref[...] * conv).astype(BF16)
    y_conv = jnp.dot(z, wco_ref[...], preferred_element_type=F32)
    y_attn = jnp.dot(o_ref[...], wao_ref[...], preferred_element_type=F32)
    merged = (gate[:, :D_MODEL] * y_conv + gate[:, D_MODEL:] * y_attn).astype(BF16)
    m = jnp.dot(merged, wo_ref[...], preferred_element_type=F32)
    out_ref[...] = h + _rms(m, g_ref[...])


def _mix(inner, bc, o, h, g_in, wg, cw, bg, wco, wao, wo, g, seq):
    n_tok = h.shape[0]
    tm = ROW_TILE
    hb = tm // V7X_SUBLANES
    n_hb = n_tok // V7X_SUBLANES
    grid = (n_tok // tm,)
    row = lambda i: (i, 0)
    const = lambda i: (0, 0)
    in_specs = [
        pl.BlockSpec((tm, CONV_WIDTH), row),
        pl.BlockSpec((V7X_SUBLANES, CONV_WIDTH), lambda i: (jnp.maximum(i * hb - 1, 0), 0)),
        pl.BlockSpec((V7X_SUBLANES, CONV_WIDTH), lambda i: (jnp.minimum((i + 1) * hb, n_hb - 1), 0)),
        pl.BlockSpec((tm, CONV_WIDTH), row),
        pl.BlockSpec((tm, ATTN_WIDTH), row),
        pl.BlockSpec((tm, D_MODEL), row),
        pl.BlockSpec((1, D_MODEL), const),
        pl.BlockSpec((D_MODEL, 2 * D_MODEL), const),
        pl.BlockSpec((CONV_K, CONV_WIDTH), const),
        pl.BlockSpec((1, 2 * D_MODEL), const),
        pl.BlockSpec((CONV_WIDTH, D_MODEL), const),
        pl.BlockSpec((ATTN_WIDTH, D_MODEL), const),
        pl.BlockSpec((D_MODEL, D_MODEL), const),
        pl.BlockSpec((1, D_MODEL), const),
    ]
    blocks = (2 * _nbytes((tm, CONV_WIDTH), F32) + _nbytes((tm, ATTN_WIDTH), BF16)
              + 2 * _nbytes((tm, D_MODEL), F32) + _nbytes((D_MODEL, 2 * D_MODEL), BF16)
              + _nbytes((2 * CONV_WIDTH + D_MODEL, D_MODEL), BF16) + 8 * _nbytes((8, 2 * D_MODEL), F32))
    temps = 10 * _nbytes((tm, D_MODEL), F32)
    return pl.pallas_call(
        functools.partial(_mix_kernel, tiles_per_seq=seq // tm),
        grid=grid, in_specs=in_specs, out_specs=pl.BlockSpec((tm, D_MODEL), row),
        out_shape=jax.ShapeDtypeStruct((n_tok, D_MODEL), F32),
        compiler_params=pltpu.CompilerParams(
            dimension_semantics=("arbitrary",), vmem_limit_bytes=_vmem_limit(blocks, temps)),
        name="mix",
    )(inner, inner, inner, bc, o, h, g_in, wg, cw, bg, wco, wao, wo, g)


FF_CHUNK = 1024


def _ffn_kernel(h_ref, g_in_ref, w1_ref, w2_ref, g_out_ref, out_ref):
    x = h_ref[...]
    u = _rms(x, g_in_ref[...]).astype(BF16)
    acc = jnp.zeros(x.shape, F32)
    for lo in range(0, D_FF, FF_CHUNK):
        hid = jnp.dot(u, w1_ref[:, lo:lo + FF_CHUNK], preferred_element_type=F32)
        hid = jnp.square(jnp.maximum(hid, 0.0)).astype(BF16)
        acc = acc + jnp.dot(hid, w2_ref[lo:lo + FF_CHUNK, :], preferred_element_type=F32)
    out_ref[...] = x + _rms(acc, g_out_ref[...])


def _ffn(h, g_in, w1, w2, g_out):
    n_tok = h.shape[0]
    tm = ROW_TILE
    row = lambda i: (i, 0)
    const = lambda i: (0, 0)
    in_specs = [
        pl.BlockSpec((tm, D_MODEL), row),
        pl.BlockSpec((1, D_MODEL), const),
        pl.BlockSpec((D_MODEL, D_FF), const),
        pl.BlockSpec((D_FF, D_MODEL), const),
        pl.BlockSpec((1, D_MODEL), const),
    ]
    blocks = 2 * _nbytes((tm, D_MODEL), F32) + 2 * _nbytes((D_MODEL, D_FF), BF16)
    temps = 6 * _nbytes((tm, D_MODEL), F32)
    return pl.pallas_call(
        _ffn_kernel, grid=(n_tok // tm,), in_specs=in_specs,
        out_specs=pl.BlockSpec((tm, D_MODEL), row),
        out_shape=jax.ShapeDtypeStruct((n_tok, D_MODEL), F32),
        compiler_params=pltpu.CompilerParams(
            dimension_semantics=("arbitrary",), vmem_limit_bytes=_vmem_limit(blocks, temps)),
        name="ffn",
    )(h, g_in, w1, w2, g_out)


def _alibi_consts():
    slopes = jnp.exp2(-8.0 * jnp.arange(1, N_HEADS + 1, dtype=F32) / N_HEADS)
    cs = slopes * LOG2E
    x = cs[None, :] * jnp.arange(K_TILE, dtype=F32)[:, None]
    terms, rest = [], x
    for _ in range(N_AUX):
        t = rest.astype(BF16).astype(F32)
        terms.append(t)
        rest = rest - t
    aux = jnp.stack(terms, axis=-1)
    lanes = jnp.zeros((K_TILE, N_HEADS, 2, K_LANES), F32)
    lanes = lanes.at[:, :, :, HEAD_DIM:HEAD_DIM + N_AUX].set(aux[:, :, None, :])
    n_pos = K_TILE // Q_TILE
    qpos = jnp.arange(Q_TILE)[None, None, :] + Q_TILE * jnp.arange(n_pos)[:, None, None]
    dist = jnp.abs(qpos - jnp.arange(K_TILE)[None, :, None]).astype(F32)
    return jnp.stack([cs, 1.0 / cs]), lanes.reshape(K_TILE, 2 * N_HEADS * K_LANES), dist


def _prep_in_weights(w_in):
    cw, aw = CONV_WIDTH, ATTN_WIDTH
    w_conv = w_in[:, :3 * cw]
    w_q = w_in[:, 3 * cw:3 * cw + aw]
    w_k = w_in[:, 3 * cw + aw:3 * cw + 2 * aw]
    w_v = w_in[:, 3 * cw + 2 * aw:3 * cw + 3 * aw]
    w_g = w_in[:, 3 * cw + 3 * aw:]
    w_kp = jnp.pad(w_k.reshape(D_MODEL, 2 * N_HEADS, HEAD_DIM),
                   ((0, 0), (0, 0), (0, K_LANES - HEAD_DIM))).reshape(D_MODEL, -1)
    wnn = jnp.concatenate([w_conv, w_kp], axis=1).astype(BF16)
    wt = jnp.concatenate([w_q, w_v], axis=1).T.astype(BF16)
    return wnn, wt, w_g.astype(BF16)


def kernel(x, w_in, b_gate, conv_w, w_conv_out, lambda_qk, subln_g, w_attn_out, w_o, norm_g,
           w_ff1, w_ff2):
    batch, seq, _ = x.shape
    depth = w_in.shape[0]
    assert seq % ROW_TILE == 0 and seq % Q_TILE == 0 and K_TILE % Q_TILE == 0
    cs, kaux, dist = _alibi_consts()
    h = x.reshape(batch * seq, D_MODEL)
    for l in range(depth):
        lam_init = 0.8 - 0.6 * math.exp(-0.3 * l)
        wnn, wt, wg = _prep_in_weights(w_in[l])
        inner, bc, kp, qt, vt, kn2 = _inproj(h, norm_g[l, 0:1], wnn, wt, kaux, batch, seq)
        o = _attention(cs, lambda_qk[l], subln_g[l].reshape(V_DIM, 1), dist, kn2, qt, kp, vt,
                       lam_init, batch, seq)
        h = _mix(inner, bc, o.reshape(batch * seq, ATTN_WIDTH), h, norm_g[l, 0:1], wg, conv_w[l],
                 b_gate[l].reshape(1, -1), w_conv_out[l].astype(BF16), w_attn_out[l].astype(BF16),
                 w_o[l].astype(BF16), norm_g[l, 1:2], seq)
        h = _ffn(h, norm_g[l, 2:3], w_ff1[l].astype(BF16), w_ff2[l].astype(BF16), norm_g[l, 3:4])
    return h.reshape(batch, seq, D_MODEL)
```

```python
import functools
import math

import jax
import jax.numpy as jnp
from jax import lax
from jax.experimental import pallas as pl
from jax.experimental.pallas import tpu as pltpu

D_MODEL = 1024
CONV_WIDTH = 512
CONV_K = 3
N_HEADS = 4
HEAD_DIM = 64
V_DIM = 128
ATTN_WIDTH = N_HEADS * V_DIM
D_FF = 4 * D_MODEL
EPS = 1e-6

F32 = jnp.float32
BF16 = jnp.bfloat16

V7X_LANES = 128
V7X_SUBLANES = 8
V7X_BF16_ROWS_PER_VREG = 16
V7X_MXU_DIM = 256
V7X_VMEM_BYTES = 64 * 1024 * 1024

ROW_TILE = 512
Q_TILE = 2 * V7X_MXU_DIM
K_TILE = ROW_TILE
K_LANES = 2 * HEAD_DIM
N_AUX = 3
V_ROWS = V_DIM + V7X_BF16_ROWS_PER_VREG
LOG2E = math.log2(math.e)
Q_SCALE = HEAD_DIM ** -0.5 * LOG2E
M_INIT = -1e30
STEPS_PER_ITER = 6
SKIP_MARGIN = 152.0
NORM_SLACK = 1.01

W_NN_COLS = 3 * CONV_WIDTH + 2 * N_HEADS * K_LANES
OFF_BC, OFF_CC, OFF_K = 512, 1024, 1536


def _vmem_limit(block_bytes, temp_bytes):
    need = 2 * block_bytes + temp_bytes
    assert need < V7X_VMEM_BYTES, need
    return int(need)


def _nbytes(shape, dtype):
    return math.prod(shape) * jnp.dtype(dtype).itemsize


def _layer_spec(shape, index):
    return pl.BlockSpec((None,) + shape, lambda *_: (index,) + (0,) * len(shape))


def _rms(x, g):
    return x * lax.rsqrt(jnp.mean(x * x, axis=-1, keepdims=True) + EPS) * g


def _inproj_kernel(h_ref, g_ref, wnn_ref, wt_ref, kaux_ref,
                   inner_ref, bc_ref, kp_ref, qt_ref, vt_ref, kn2_ref):
    u = _rms(h_ref[...], g_ref[...]).astype(BF16)

    def nn(lo, hi):
        return jnp.dot(u, wnn_ref[:, lo:hi], preferred_element_type=F32)

    inner_ref[...] = nn(OFF_CC, OFF_K) * nn(0, OFF_BC)
    bc_ref[...] = nn(OFF_BC, OFF_CC)
    kf = nn(OFF_K, W_NN_COLS)
    kp_ref[...] = (kf + kaux_ref[...]).astype(BF16)
    for hm in range(2 * N_HEADS):
        sq = jnp.sum(jnp.square(kf[:, hm * K_LANES:(hm + 1) * K_LANES]), axis=1, keepdims=True)
        kn2_ref[0, hm:hm + 1, :] = jnp.broadcast_to(
            jnp.max(sq, axis=0, keepdims=True), (1, V7X_LANES))

    qv = lax.dot_general(wt_ref[...], u, (((1,), (1,)), ((), ())),
                         preferred_element_type=F32)
    qt_ref[0] = (qv[:ATTN_WIDTH] * Q_SCALE).astype(BF16)
    tm = u.shape[0]
    ones_rows = jnp.where(
        lax.broadcasted_iota(jnp.int32, (V_ROWS - V_DIM, tm), 0) == 0, 1.0, 0.0).astype(BF16)
    for h in range(N_HEADS):
        lo = ATTN_WIDTH + h * V_DIM
        vt_ref[0, h, 0, :V_DIM, :] = qv[lo:lo + V_DIM].astype(BF16)
        vt_ref[0, h, 0, V_DIM:, :] = ones_rows


def _inproj(h, norm_g, wnn, wt, kaux, layer, batch, seq):
    n_tok = h.shape[0]
    tm = ROW_TILE
    tpb = seq // tm
    grid = (n_tok // tm,)
    row = lambda i: (i, 0)
    const = lambda i: (0, 0)
    out_shape = (
        jax.ShapeDtypeStruct((n_tok, CONV_WIDTH), F32),
        jax.ShapeDtypeStruct((n_tok, CONV_WIDTH), F32),
        jax.ShapeDtypeStruct((n_tok, 2 * N_HEADS * K_LANES), BF16),
        jax.ShapeDtypeStruct((batch, ATTN_WIDTH, seq), BF16),
        jax.ShapeDtypeStruct((batch, N_HEADS, tpb, V_ROWS, tm), BF16),
        jax.ShapeDtypeStruct((n_tok // tm, 2 * N_HEADS, V7X_LANES), F32),
    )
    in_specs = [
        pl.BlockSpec((tm, D_MODEL), row),
        _layer_spec((1, D_MODEL), 4 * layer),
        _layer_spec((D_MODEL, W_NN_COLS), layer),
        _layer_spec((2 * ATTN_WIDTH, D_MODEL), layer),
        pl.BlockSpec((tm, 2 * N_HEADS * K_LANES), const),
    ]
    out_specs = (
        pl.BlockSpec((tm, CONV_WIDTH), row),
        pl.BlockSpec((tm, CONV_WIDTH), row),
        pl.BlockSpec((tm, 2 * N_HEADS * K_LANES), row),
        pl.BlockSpec((1, ATTN_WIDTH, tm), lambda i: (i // tpb, 0, i % tpb)),
        pl.BlockSpec((1, N_HEADS, 1, V_ROWS, tm), lambda i: (i // tpb, 0, i % tpb, 0, 0)),
        pl.BlockSpec((1, 2 * N_HEADS, V7X_LANES), lambda i: (i, 0, 0)),
    )
    blocks = (_nbytes((tm, D_MODEL), F32) + _nbytes((D_MODEL, W_NN_COLS), BF16)
              + _nbytes((2 * ATTN_WIDTH, D_MODEL), BF16) + _nbytes((tm, 1024), F32)
              + 2 * _nbytes((tm, CONV_WIDTH), F32)
              + _nbytes((tm, 1024), BF16) + _nbytes((ATTN_WIDTH, tm), BF16)
              + _nbytes((N_HEADS, V_ROWS, tm), BF16))
    temps = 4 * _nbytes((tm, D_MODEL), F32) + _nbytes((2 * ATTN_WIDTH, tm), F32)
    return pl.pallas_call(
        _inproj_kernel, grid=grid, in_specs=in_specs, out_specs=out_specs, out_shape=out_shape,
        compiler_params=pltpu.CompilerParams(
            dimension_semantics=("arbitrary",), vmem_limit_bytes=_vmem_limit(blocks, temps)),
        name="inproj",
    )(h, norm_g, wnn, wt, kaux)


def _attn_kernel(cs_ref, lq_ref, g_ref, dist_ref, kn2_ref, qt_ref, kp_ref, vt_ref, o_ref,
                 qv_ref, sa_ref, sb_ref, acc_ref, *, lam_init):
    h = pl.program_id(1)
    qi = pl.program_id(2)
    tq, tk = Q_TILE, K_TILE
    n_kb = kp_ref.shape[1] // tk
    c = cs_ref[0, h]
    q0 = qi * tq
    kd = q0 // tk

    aux_rows = lax.broadcasted_iota(jnp.int32, (K_LANES - HEAD_DIM, tq), 0) < N_AUX
    for v, sign in enumerate((1.0, -1.0, 0.0)):
        for m in range(2):
            qv_ref[v, m, :HEAD_DIM, :] = qt_ref[0, m * HEAD_DIM:(m + 1) * HEAD_DIM, :]
            qv_ref[v, m, HEAD_DIM:, :] = jnp.where(aux_rows, sign, 0.0).astype(BF16)

    ci = c * (q0 + lax.broadcasted_iota(jnp.int32, (1, tq), 1)).astype(F32)
    acc_ref[...] = jnp.zeros_like(acc_ref)

    def scores(kb, variant, s_ref, bias):
        kblk = kp_ref[0, pl.ds(pl.multiple_of(kb * tk, tk), tk), :]
        cms = []
        for m in range(2):
            st = jnp.dot(kblk[:, m * K_LANES:(m + 1) * K_LANES], qv_ref[variant, m],
                         preferred_element_type=F32)
            if bias is not None:
                st = st + bias
            s_ref[m] = st
            cms.append(jnp.max(st, axis=0, keepdims=True))
        return tuple(cms)

    def fold(kb, s_ref, cms, off, ms):
        vblk = vt_ref[0, 0, kb]
        new_ms = []
        for m in range(2):
            m_new = jnp.maximum(ms[m], cms[m] + off)
            alpha = jnp.exp2(ms[m] - m_new)
            p = jnp.exp2(s_ref[m] - (m_new - off)).astype(BF16)
            acc_ref[m] = acc_ref[m] * alpha + jnp.dot(vblk, p, preferred_element_type=F32)
            new_ms.append(m_new)
        return tuple(new_ms)

    def block_of(i, n_left):
        kb = jnp.where(i < n_left, kd - 1 - i, kd + 1 + i - n_left)
        left = kb < kd
        off = jnp.where(left, 1.0, -1.0) * (c * (kb * tk).astype(F32) - ci)
        return kb, jnp.where(left, 0, 1), off

    cms = scores(kd, 2, sa_ref, dist_ref[qi % (tk // tq)] * (-c))
    m0 = jnp.full((1, tq), M_INIT, F32)
    z0 = jnp.zeros((1, tq), F32)
    assert n_kb >= 2
    kb0, var0, off0 = block_of(jnp.int32(0), jnp.minimum(kd, 1))
    cms0 = scores(kb0, var0, sb_ref, None)
    ms = fold(kd, sa_ref, cms, z0, (m0, m0))

    need = None
    for m in range(2):
        qf = qt_ref[0, m * HEAD_DIM:(m + 1) * HEAD_DIM, :].astype(F32)
        qn = jnp.sqrt(jnp.max(jnp.sum(qf * qf, axis=0, keepdims=True), axis=1, keepdims=True))
        kn = jnp.sqrt(jnp.max(jnp.max(kn2_ref[0, 0, m], axis=0, keepdims=True), axis=1, keepdims=True))
        need_m = qn * kn * NORM_SLACK - jnp.min(cms[m], axis=1, keepdims=True)
        need = need_m if need is None else jnp.maximum(need, need_m)
    radius = (need + SKIP_MARGIN) * cs_ref[1, h] + 1.0
    q0f, kdf = q0.astype(F32), kd.astype(F32)
    lo = jnp.clip(jnp.floor((q0f - (tk - 1) - radius) / tk), 0.0, kdf).astype(jnp.int32)[0, 0]
    hi = jnp.clip(jnp.ceil((radius + q0f + (tq - 1)) / tk), kdf, n_kb - 1.0).astype(jnp.int32)[0, 0]
    lo = jnp.minimum(jnp.clip(lo, 0, kd), jnp.maximum(kd - 1, 0))
    hi = jnp.maximum(jnp.clip(hi, kd, n_kb - 1), jnp.minimum(kd + 1, n_kb - 1))
    n_left = kd - lo
    n_steps = hi - lo

    def step(i, s_next, s_prev, carry):
        ms, cms_prev, kb_prev, off_prev = carry
        kb, variant, off = block_of(i, n_left)
        cms_next = scores(kb, variant, s_next, None)
        ms = fold(kb_prev, s_prev, cms_prev, off_prev, ms)
        return ms, cms_next, kb, off

    bufs = (sa_ref, sb_ref)

    def steps(first, n, carry):
        for u in range(n):
            carry = step(first + u, bufs[u % 2], bufs[(u + 1) % 2], carry)
        return carry

    carry = (ms, cms0, kb0, off0)
    n_main = (n_steps - 1) // STEPS_PER_ITER
    carry = lax.fori_loop(
        0, n_main, lambda j, carry: steps(1 + j * STEPS_PER_ITER, STEPS_PER_ITER, carry), carry)
    done = 1 + n_main * STEPS_PER_ITER
    n_pairs = (n_steps - done) // 2
    carry = lax.fori_loop(0, n_pairs, lambda j, carry: steps(done + 2 * j, 2, carry), carry)
    done = done + 2 * n_pairs

    def tail_odd(carry):
        ms, cms, kb_last, off_last = steps(done, 1, carry)
        fold(kb_last, bufs[0], cms, off_last, ms)

    def tail_even(carry):
        ms, cms, kb_last, off_last = carry
        fold(kb_last, bufs[1], cms, off_last, ms)

    lax.cond(n_steps - done == 1, tail_odd, tail_even, carry)

    lq = lq_ref[...]
    lam = (jnp.exp(jnp.sum(lq[0:1] * lq[1:2], axis=-1, keepdims=True))
           - jnp.exp(jnp.sum(lq[2:3] * lq[3:4], axis=-1, keepdims=True)) + lam_init)
    o1 = acc_ref[0, :V_DIM, :] / acc_ref[0, V_DIM:V_DIM + 1, :]
    o2 = acc_ref[1, :V_DIM, :] / acc_ref[1, V_DIM:V_DIM + 1, :]
    ot = o1 - lam * o2
    y = ot * lax.rsqrt(jnp.mean(ot * ot, axis=0, keepdims=True) + EPS) * g_ref[...]
    y = y * (1.0 - lam_init)
    o_ref[0] = y.T.astype(BF16)


def _attention(cs, lq, g_col, dist, kn2, qt, kp, vt, layer, lam_init, batch, seq):
    tq, tk = Q_TILE, K_TILE
    grid = (batch, N_HEADS, seq // tq)
    kn2 = kn2.reshape(batch, seq // tk, N_HEADS, 2, V7X_LANES).transpose(0, 2, 3, 1, 4)
    in_specs = [
        pl.BlockSpec(memory_space=pltpu.SMEM),
        _layer_spec((4, HEAD_DIM), layer),
        _layer_spec((V_DIM, 1), layer),
        pl.BlockSpec((tk // tq, tk, tq), lambda b, h, q: (0, 0, 0)),
        pl.BlockSpec((1, 1, 2, seq // tk, V7X_LANES), lambda b, h, q: (b, h, 0, 0, 0)),
        pl.BlockSpec((1, 2 * HEAD_DIM, tq), lambda b, h, q: (b, h, q)),
        pl.BlockSpec((1, seq, 2 * K_LANES), lambda b, h, q: (b, 0, h)),
        pl.BlockSpec((1, 1, seq // tk, V_ROWS, tk), lambda b, h, q: (b, h, 0, 0, 0)),
    ]
    out_specs = pl.BlockSpec((1, tq, V_DIM), lambda b, h, q: (b, q, h))
    blocks = (_nbytes((seq, 2 * K_LANES), BF16) + _nbytes((seq // tk, V_ROWS, tk), BF16)
              + _nbytes((2 * HEAD_DIM, tq), BF16) + _nbytes((tq, V_DIM), BF16)
              + _nbytes((V_DIM, V7X_LANES), F32) + _nbytes((V7X_SUBLANES, V7X_LANES), F32)
              + _nbytes((tk // tq, tk, tq), F32))
    temps = 2 * _nbytes((V_ROWS, tq), F32) + 12 * _nbytes((tk, tq), F32)
    return pl.pallas_call(
        functools.partial(_attn_kernel, lam_init=lam_init),
        grid=grid, in_specs=in_specs, out_specs=out_specs,
        out_shape=jax.ShapeDtypeStruct((batch, seq, ATTN_WIDTH), BF16),
        scratch_shapes=[pltpu.VMEM((3, 2, K_LANES, tq), BF16),
                        pltpu.VMEM((2, tk, tq), F32),
                        pltpu.VMEM((2, tk, tq), F32),
                        pltpu.VMEM((2, V_ROWS, tq), F32)],
        compiler_params=pltpu.CompilerParams(
            dimension_semantics=("arbitrary", "arbitrary", "arbitrary"),
            vmem_limit_bytes=_vmem_limit(blocks, temps)),
        name="attn",
    )(cs, lq, g_col, dist, kn2, qt, kp.reshape(batch, seq, -1), vt)


def _mix_kernel(inner_ref, prev_ref, next_ref, bc_ref, o_ref, h_ref, g_in_ref, wg_ref,
                cw_ref, bg_ref, wco_ref, wao_ref, wo_ref, g_ref, out_ref, *, tiles_per_seq):
    t = pl.program_id(0) % tiles_per_seq
    h = h_ref[...]
    u = _rms(h, g_in_ref[...]).astype(BF16)
    gate = jax.nn.sigmoid(jnp.dot(u, wg_ref[...], preferred_element_type=F32) + bg_ref[...])
    x = inner_ref[...]
    tm = x.shape[0]
    prev_row = jnp.where(t == 0, 0.0, prev_ref[V7X_SUBLANES - 1:V7X_SUBLANES, :])
    next_row = jnp.where(t == tiles_per_seq - 1, 0.0, next_ref[0:1, :])
    row = lax.broadcasted_iota(jnp.int32, x.shape, 0)
    up = jnp.where(row == 0, prev_row, pltpu.roll(x, 1, 0))
    dn = jnp.where(row == tm - 1, next_row, pltpu.roll(x, tm - 1, 0))
    conv = up * cw_ref[0:1, :] + x * cw_ref[1:2, :] + dn * cw_ref[2:3, :]
    z = (bc_ref[...] * conv).astype(BF16)
    y_conv = jnp.dot(z, wco_ref[...], preferred_element_type=F32)
    y_attn = jnp.dot(o_ref[...], wao_ref[...], preferred_element_type=F32)
    merged = (gate[:, :D_MODEL] * y_conv + gate[:, D_MODEL:] * y_attn).astype(BF16)
    m = jnp.dot(merged, wo_ref[...], preferred_element_type=F32)
    out_ref[...] = h + _rms(m, g_ref[...])


def _mix(inner, bc, o, h, norm_g, wg, cw, bg, wco, wao, wo, layer, seq):
    n_tok = h.shape[0]
    tm = ROW_TILE
    hb = tm // V7X_SUBLANES
    n_hb = n_tok // V7X_SUBLANES
    grid = (n_tok // tm,)
    row = lambda i: (i, 0)
    in_specs = [
        pl.BlockSpec((tm, CONV_WIDTH), row),
        pl.BlockSpec((V7X_SUBLANES, CONV_WIDTH), lambda i: (jnp.maximum(i * hb - 1, 0), 0)),
        pl.BlockSpec((V7X_SUBLANES, CONV_WIDTH), lambda i: (jnp.minimum((i + 1) * hb, n_hb - 1), 0)),
        pl.BlockSpec((tm, CONV_WIDTH), row),
        pl.BlockSpec((tm, ATTN_WIDTH), row),
        pl.BlockSpec((tm, D_MODEL), row),
        _layer_spec((1, D_MODEL), 4 * layer),
        _layer_spec((D_MODEL, 2 * D_MODEL), layer),
        _layer_spec((CONV_K, CONV_WIDTH), layer),
        _layer_spec((1, 2 * D_MODEL), layer),
        _layer_spec((CONV_WIDTH, D_MODEL), layer),
        _layer_spec((ATTN_WIDTH, D_MODEL), layer),
        _layer_spec((D_MODEL, D_MODEL), layer),
        _layer_spec((1, D_MODEL), 4 * layer + 1),
    ]
    blocks = (2 * _nbytes((tm, CONV_WIDTH), F32) + _nbytes((tm, ATTN_WIDTH), BF16)
              + 2 * _nbytes((tm, D_MODEL), F32) + _nbytes((D_MODEL, 2 * D_MODEL), BF16)
              + _nbytes((2 * CONV_WIDTH + D_MODEL, D_MODEL), BF16) + 8 * _nbytes((8, 2 * D_MODEL), F32))
    temps = 10 * _nbytes((tm, D_MODEL), F32)
    return pl.pallas_call(
        functools.partial(_mix_kernel, tiles_per_seq=seq // tm),
        grid=grid, in_specs=in_specs, out_specs=pl.BlockSpec((tm, D_MODEL), row),
        out_shape=jax.ShapeDtypeStruct((n_tok, D_MODEL), F32),
        compiler_params=pltpu.CompilerParams(
            dimension_semantics=("arbitrary",), vmem_limit_bytes=_vmem_limit(blocks, temps)),
        name="mix",
    )(inner, inner, inner, bc, o, h, norm_g, wg, cw, bg, wco, wao, wo, norm_g)


FF_CHUNK = 1024


def _ffn_kernel(h_ref, g_in_ref, w1_ref, w2_ref, g_out_ref, out_ref):
    x = h_ref[...]
    u = _rms(x, g_in_ref[...]).astype(BF16)
    acc = jnp.zeros(x.shape, F32)
    for lo in range(0, D_FF, FF_CHUNK):
        hid = jnp.dot(u, w1_ref[:, lo:lo + FF_CHUNK], preferred_element_type=F32)
        hid = jnp.square(jnp.maximum(hid, 0.0)).astype(BF16)
        acc = acc + jnp.dot(hid, w2_ref[lo:lo + FF_CHUNK, :], preferred_element_type=F32)
    out_ref[...] = x + _rms(acc, g_out_ref[...])


def _ffn(h, norm_g, w1, w2, layer):
    n_tok = h.shape[0]
    tm = ROW_TILE
    row = lambda i: (i, 0)
    in_specs = [
        pl.BlockSpec((tm, D_MODEL), row),
        _layer_spec((1, D_MODEL), 4 * layer + 2),
        _layer_spec((D_MODEL, D_FF), layer),
        _layer_spec((D_FF, D_MODEL), layer),
        _layer_spec((1, D_MODEL), 4 * layer + 3),
    ]
    blocks = 2 * _nbytes((tm, D_MODEL), F32) + 2 * _nbytes((D_MODEL, D_FF), BF16)
    temps = 6 * _nbytes((tm, D_MODEL), F32)
    return pl.pallas_call(
        _ffn_kernel, grid=(n_tok // tm,), in_specs=in_specs,
        out_specs=pl.BlockSpec((tm, D_MODEL), row),
        out_shape=jax.ShapeDtypeStruct((n_tok, D_MODEL), F32),
        compiler_params=pltpu.CompilerParams(
            dimension_semantics=("arbitrary",), vmem_limit_bytes=_vmem_limit(blocks, temps)),
        name="ffn",
    )(h, norm_g, w1, w2, norm_g)


def _alibi_consts():
    slopes = jnp.exp2(-8.0 * jnp.arange(1, N_HEADS + 1, dtype=F32) / N_HEADS)
    cs = slopes * LOG2E
    x = cs[None, :] * jnp.arange(K_TILE, dtype=F32)[:, None]
    terms, rest = [], x
    for _ in range(N_AUX):
        t = rest.astype(BF16).astype(F32)
        terms.append(t)
        rest = rest - t
    aux = jnp.stack(terms, axis=-1)
    lanes = jnp.zeros((K_TILE, N_HEADS, 2, K_LANES), F32)
    lanes = lanes.at[:, :, :, HEAD_DIM:HEAD_DIM + N_AUX].set(aux[:, :, None, :])
    n_pos = K_TILE // Q_TILE
    qpos = jnp.arange(Q_TILE)[None, None, :] + Q_TILE * jnp.arange(n_pos)[:, None, None]
    dist = jnp.abs(qpos - jnp.arange(K_TILE)[None, :, None]).astype(F32)
    return jnp.stack([cs, 1.0 / cs]), lanes.reshape(K_TILE, 2 * N_HEADS * K_LANES), dist


def _prep_in_weights(w_in):
    cw, aw = CONV_WIDTH, ATTN_WIDTH
    depth = w_in.shape[0]
    w_conv = w_in[:, :, :3 * cw]
    w_q = w_in[:, :, 3 * cw:3 * cw + aw]
    w_k = w_in[:, :, 3 * cw + aw:3 * cw + 2 * aw]
    w_v = w_in[:, :, 3 * cw + 2 * aw:3 * cw + 3 * aw]
    w_g = w_in[:, :, 3 * cw + 3 * aw:]
    w_kp = jnp.pad(w_k.reshape(depth, D_MODEL, 2 * N_HEADS, HEAD_DIM),
                   ((0, 0), (0, 0), (0, 0), (0, K_LANES - HEAD_DIM))).reshape(depth, D_MODEL, -1)
    wnn = jnp.concatenate([w_conv, w_kp], axis=2).astype(BF16)
    wt = jnp.swapaxes(jnp.concatenate([w_q, w_v], axis=2), 1, 2).astype(BF16)
    return wnn, wt, w_g.astype(BF16)


def kernel(x, w_in, b_gate, conv_w, w_conv_out, lambda_qk, subln_g, w_attn_out, w_o, norm_g,
           w_ff1, w_ff2):
    batch, seq, _ = x.shape
    depth = w_in.shape[0]
    assert seq % ROW_TILE == 0 and seq % Q_TILE == 0 and K_TILE % Q_TILE == 0
    cs, kaux, dist = _alibi_consts()
    wnn, wt, wg = _prep_in_weights(w_in)
    wco, wao, wo = w_conv_out.astype(BF16), w_attn_out.astype(BF16), w_o.astype(BF16)
    w1, w2 = w_ff1.astype(BF16), w_ff2.astype(BF16)
    norm_g = norm_g.reshape(depth * 4, 1, D_MODEL)
    bg = b_gate.reshape(depth, 1, 2 * D_MODEL)
    g_col = subln_g.reshape(depth, V_DIM, 1)
    h = x.reshape(batch * seq, D_MODEL)
    for l in range(depth):
        lam_init = 0.8 - 0.6 * math.exp(-0.3 * l)
        inner, bc, kp, qt, vt, kn2 = _inproj(h, norm_g, wnn, wt, kaux, l, batch, seq)
        o = _attention(cs, lambda_qk, g_col, dist, kn2, qt, kp, vt, l, lam_init, batch, seq)
        h = _mix(inner, bc, o.reshape(batch * seq, ATTN_WIDTH), h, norm_g, wg, conv_w, bg,
                 wco, wao, wo, l, seq)
        h = _ffn(h, norm_g, w1, w2, l)
    return h.reshape(batch, seq, D_MODEL)
```

```python
import functools
import math

import jax
import jax.numpy as jnp
from jax import lax
from jax.experimental import pallas as pl
from jax.experimental.pallas import tpu as pltpu

D_MODEL = 1024
CONV_WIDTH = 512
CONV_K = 3
N_HEADS = 4
HEAD_DIM = 64
V_DIM = 128
ATTN_WIDTH = N_HEADS * V_DIM
D_FF = 4 * D_MODEL
EPS = 1e-6

F32 = jnp.float32
BF16 = jnp.bfloat16

V7X_LANES = 128
V7X_SUBLANES = 8
V7X_BF16_ROWS_PER_VREG = 16
V7X_MXU_DIM = 256
V7X_VMEM_BYTES = 64 * 1024 * 1024

ROW_TILE = 512
Q_TILE = 2 * V7X_MXU_DIM
K_TILE = ROW_TILE
K_LANES = 2 * HEAD_DIM
N_AUX = 3
V_ROWS = V_DIM + V7X_BF16_ROWS_PER_VREG
LOG2E = math.log2(math.e)
Q_SCALE = HEAD_DIM ** -0.5 * LOG2E
M_INIT = -1e30
STEPS_PER_ITER = 6
SKIP_MARGIN = 152.0
NORM_SLACK = 1.01

W_NN_COLS = 3 * CONV_WIDTH + 2 * N_HEADS * K_LANES
OFF_BC, OFF_CC, OFF_K = 512, 1024, 1536


def _vmem_limit(block_bytes, temp_bytes):
    need = 2 * block_bytes + temp_bytes
    assert need < V7X_VMEM_BYTES, need
    return int(need)


def _nbytes(shape, dtype):
    return math.prod(shape) * jnp.dtype(dtype).itemsize


def _layer_spec(shape, index):
    return pl.BlockSpec((None,) + shape, lambda *_: (index,) + (0,) * len(shape))


def _rms(x, g):
    return x * lax.rsqrt(jnp.mean(x * x, axis=-1, keepdims=True) + EPS) * g


def _inproj_kernel(h_ref, g_ref, wnn_ref, wt_ref, kaux_ref,
                   inner_ref, bc_ref, kp_ref, qt_ref, vt_ref, kn2_ref):
    u = _rms(h_ref[...], g_ref[...]).astype(BF16)

    def nn(lo, hi):
        return jnp.dot(u, wnn_ref[:, lo:hi], preferred_element_type=F32)

    inner_ref[...] = nn(OFF_CC, OFF_K) * nn(0, OFF_BC)
    bc_ref[...] = nn(OFF_BC, OFF_CC)
    kf = nn(OFF_K, W_NN_COLS)
    kp_ref[...] = (kf + kaux_ref[...]).astype(BF16)
    for hm in range(2 * N_HEADS):
        sq = jnp.sum(jnp.square(kf[:, hm * K_LANES:(hm + 1) * K_LANES]), axis=1, keepdims=True)
        kn2_ref[0, hm:hm + 1, :] = jnp.broadcast_to(
            jnp.max(sq, axis=0, keepdims=True), (1, V7X_LANES))

    qv = lax.dot_general(wt_ref[...], u, (((1,), (1,)), ((), ())),
                         preferred_element_type=F32)
    qt_ref[0] = (qv[:ATTN_WIDTH] * Q_SCALE).astype(BF16)
    tm = u.shape[0]
    ones_rows = jnp.where(
        lax.broadcasted_iota(jnp.int32, (V_ROWS - V_DIM, tm), 0) == 0, 1.0, 0.0).astype(BF16)
    for h in range(N_HEADS):
        lo = ATTN_WIDTH + h * V_DIM
        vt_ref[0, h, 0, :V_DIM, :] = qv[lo:lo + V_DIM].astype(BF16)
        vt_ref[0, h, 0, V_DIM:, :] = ones_rows


def _inproj(h, norm_g, wnn, wt, kaux, layer, batch, seq):
    n_tok = h.shape[0]
    tm = ROW_TILE
    tpb = seq // tm
    grid = (n_tok // tm,)
    row = lambda i: (i, 0)
    const = lambda i: (0, 0)
    out_shape = (
        jax.ShapeDtypeStruct((n_tok, CONV_WIDTH), F32),
        jax.ShapeDtypeStruct((n_tok, CONV_WIDTH), F32),
        jax.ShapeDtypeStruct((n_tok, 2 * N_HEADS * K_LANES), BF16),
        jax.ShapeDtypeStruct((batch, ATTN_WIDTH, seq), BF16),
        jax.ShapeDtypeStruct((batch, N_HEADS, tpb, V_ROWS, tm), BF16),
        jax.ShapeDtypeStruct((n_tok // tm, 2 * N_HEADS, V7X_LANES), F32),
    )
    in_specs = [
        pl.BlockSpec((tm, D_MODEL), row),
        _layer_spec((1, D_MODEL), 4 * layer),
        _layer_spec((D_MODEL, W_NN_COLS), layer),
        _layer_spec((2 * ATTN_WIDTH, D_MODEL), layer),
        pl.BlockSpec((tm, 2 * N_HEADS * K_LANES), const),
    ]
    out_specs = (
        pl.BlockSpec((tm, CONV_WIDTH), row),
        pl.BlockSpec((tm, CONV_WIDTH), row),
        pl.BlockSpec((tm, 2 * N_HEADS * K_LANES), row),
        pl.BlockSpec((1, ATTN_WIDTH, tm), lambda i: (i // tpb, 0, i % tpb)),
        pl.BlockSpec((1, N_HEADS, 1, V_ROWS, tm), lambda i: (i // tpb, 0, i % tpb, 0, 0)),
        pl.BlockSpec((1, 2 * N_HEADS, V7X_LANES), lambda i: (i, 0, 0)),
    )
    blocks = (_nbytes((tm, D_MODEL), F32) + _nbytes((D_MODEL, W_NN_COLS), BF16)
              + _nbytes((2 * ATTN_WIDTH, D_MODEL), BF16) + _nbytes((tm, 1024), F32)
              + 2 * _nbytes((tm, CONV_WIDTH), F32)
              + _nbytes((tm, 1024), BF16) + _nbytes((ATTN_WIDTH, tm), BF16)
              + _nbytes((N_HEADS, V_ROWS, tm), BF16))
    temps = 4 * _nbytes((tm, D_MODEL), F32) + _nbytes((2 * ATTN_WIDTH, tm), F32)
    return pl.pallas_call(
        _inproj_kernel, grid=grid, in_specs=in_specs, out_specs=out_specs, out_shape=out_shape,
        compiler_params=pltpu.CompilerParams(
            dimension_semantics=("arbitrary",), vmem_limit_bytes=_vmem_limit(blocks, temps)),
        name="inproj",
    )(h, norm_g, wnn, wt, kaux)


def _attn_kernel(cs_ref, lq_ref, g_ref, dist_ref, kn2_ref, qt_ref, kp_ref, vt_ref, o_ref,
                 qv_ref, sa_ref, sb_ref, acc_ref, *, lam_init):
    h = pl.program_id(1)
    qi = pl.program_id(2)
    tq, tk = Q_TILE, K_TILE
    n_kb = kp_ref.shape[1] // tk
    c = cs_ref[0, h]
    q0 = qi * tq
    kd = q0 // tk

    for v in range(3):
        for m in range(2):
            qv_ref[v, m, :HEAD_DIM, :] = qt_ref[0, m * HEAD_DIM:(m + 1) * HEAD_DIM, :]

    @pl.when((pl.program_id(0) == 0) & (h == 0) & (qi == 0))
    def _():
        aux_rows = lax.broadcasted_iota(jnp.int32, (K_LANES - HEAD_DIM, tq), 0) < N_AUX
        for v, sign in enumerate((1.0, -1.0, 0.0)):
            for m in range(2):
                qv_ref[v, m, HEAD_DIM:, :] = jnp.where(aux_rows, sign, 0.0).astype(BF16)

    ci = c * (q0 + lax.broadcasted_iota(jnp.int32, (1, tq), 1)).astype(F32)

    def scores(kb, variant, s_ref, bias):
        kblk = kp_ref[0, pl.ds(pl.multiple_of(kb * tk, tk), tk), :]
        cms = []
        for m in range(2):
            st = jnp.dot(kblk[:, m * K_LANES:(m + 1) * K_LANES], qv_ref[variant, m],
                         preferred_element_type=F32)
            if bias is not None:
                st = st + bias
            s_ref[m] = st
            cms.append(jnp.max(st, axis=0, keepdims=True))
        return tuple(cms)

    def fold(kb, s_ref, cms, off, ms, first=False):
        vblk = vt_ref[0, 0, kb]
        new_ms = []
        for m in range(2):
            m_new = jnp.maximum(ms[m], cms[m] + off)
            p = jnp.exp2(s_ref[m] - (m_new - off)).astype(BF16)
            pv = jnp.dot(vblk, p, preferred_element_type=F32)
            acc_ref[m] = pv if first else acc_ref[m] * jnp.exp2(ms[m] - m_new) + pv
            new_ms.append(m_new)
        return tuple(new_ms)

    def block_of(i, n_left):
        kb = jnp.where(i < n_left, kd - 1 - i, kd + 1 + i - n_left)
        left = kb < kd
        off = jnp.where(left, 1.0, -1.0) * (c * (kb * tk).astype(F32) - ci)
        return kb, jnp.where(left, 0, 1), off

    cms = scores(kd, 2, sa_ref, dist_ref[qi % (tk // tq)] * (-c))
    m0 = jnp.full((1, tq), M_INIT, F32)
    z0 = jnp.zeros((1, tq), F32)
    assert n_kb >= 2
    kb0, var0, off0 = block_of(jnp.int32(0), jnp.minimum(kd, 1))
    cms0 = scores(kb0, var0, sb_ref, None)
    ms = fold(kd, sa_ref, cms, z0, (m0, m0), first=True)

    need = None
    for m in range(2):
        qf = qt_ref[0, m * HEAD_DIM:(m + 1) * HEAD_DIM, :].astype(F32)
        qn = jnp.sqrt(jnp.max(jnp.sum(qf * qf, axis=0, keepdims=True), axis=1, keepdims=True))
        kn = jnp.sqrt(jnp.max(jnp.max(kn2_ref[0, 0, m], axis=0, keepdims=True), axis=1, keepdims=True))
        need_m = qn * kn * NORM_SLACK - jnp.min(cms[m], axis=1, keepdims=True)
        need = need_m if need is None else jnp.maximum(need, need_m)
    radius = (need + SKIP_MARGIN) * cs_ref[1, h] + 1.0
    q0f, kdf = q0.astype(F32), kd.astype(F32)
    lo = jnp.clip(jnp.floor((q0f - (tk - 1) - radius) / tk), 0.0, kdf).astype(jnp.int32)[0, 0]
    hi = jnp.clip(jnp.ceil((radius + q0f + (tq - 1)) / tk), kdf, n_kb - 1.0).astype(jnp.int32)[0, 0]
    lo = jnp.minimum(jnp.clip(lo, 0, kd), jnp.maximum(kd - 1, 0))
    hi = jnp.maximum(jnp.clip(hi, kd, n_kb - 1), jnp.minimum(kd + 1, n_kb - 1))
    n_left = kd - lo
    n_steps = hi - lo

    def step(i, s_next, s_prev, carry):
        ms, cms_prev, kb_prev, off_prev = carry
        kb, variant, off = block_of(i, n_left)
        cms_next = scores(kb, variant, s_next, None)
        ms = fold(kb_prev, s_prev, cms_prev, off_prev, ms)
        return ms, cms_next, kb, off

    bufs = (sa_ref, sb_ref)

    def steps(first, n, carry):
        for u in range(n):
            carry = step(first + u, bufs[u % 2], bufs[(u + 1) % 2], carry)
        return carry

    carry = (ms, cms0, kb0, off0)
    n_main = (n_steps - 1) // STEPS_PER_ITER
    carry = lax.fori_loop(
        0, n_main, lambda j, carry: steps(1 + j * STEPS_PER_ITER, STEPS_PER_ITER, carry), carry)
    done = 1 + n_main * STEPS_PER_ITER
    n_pairs = (n_steps - done) // 2
    carry = lax.fori_loop(0, n_pairs, lambda j, carry: steps(done + 2 * j, 2, carry), carry)
    done = done + 2 * n_pairs

    def tail_odd(carry):
        ms, cms, kb_last, off_last = steps(done, 1, carry)
        fold(kb_last, bufs[0], cms, off_last, ms)

    def tail_even(carry):
        ms, cms, kb_last, off_last = carry
        fold(kb_last, bufs[1], cms, off_last, ms)

    lax.cond(n_steps - done == 1, tail_odd, tail_even, carry)

    lq = lq_ref[...]
    lam = (jnp.exp(jnp.sum(lq[0:1] * lq[1:2], axis=-1, keepdims=True))
           - jnp.exp(jnp.sum(lq[2:3] * lq[3:4], axis=-1, keepdims=True)) + lam_init)
    o1 = acc_ref[0, :V_DIM, :] / acc_ref[0, V_DIM:V_DIM + 1, :]
    o2 = acc_ref[1, :V_DIM, :] / acc_ref[1, V_DIM:V_DIM + 1, :]
    ot = o1 - lam * o2
    y = ot * lax.rsqrt(jnp.mean(ot * ot, axis=0, keepdims=True) + EPS) * g_ref[...]
    y = y * (1.0 - lam_init)
    o_ref[0] = y.astype(BF16)


def _attention(cs, lq, g_col, dist, kn2, qt, kp, vt, layer, lam_init, batch, seq):
    tq, tk = Q_TILE, K_TILE
    grid = (batch, N_HEADS, seq // tq)
    kn2 = kn2.reshape(batch, seq // tk, N_HEADS, 2, V7X_LANES).transpose(0, 2, 3, 1, 4)
    in_specs = [
        pl.BlockSpec(memory_space=pltpu.SMEM),
        _layer_spec((4, HEAD_DIM), layer),
        _layer_spec((V_DIM, 1), layer),
        pl.BlockSpec((tk // tq, tk, tq), lambda b, h, q: (0, 0, 0)),
        pl.BlockSpec((1, 1, 2, seq // tk, V7X_LANES), lambda b, h, q: (b, h, 0, 0, 0)),
        pl.BlockSpec((1, 2 * HEAD_DIM, tq), lambda b, h, q: (b, h, q)),
        pl.BlockSpec((1, seq, 2 * K_LANES), lambda b, h, q: (b, 0, h)),
        pl.BlockSpec((1, 1, seq // tk, V_ROWS, tk), lambda b, h, q: (b, h, 0, 0, 0)),
    ]
    out_specs = pl.BlockSpec((1, V_DIM, tq), lambda b, h, q: (b, h, q))
    blocks = (_nbytes((seq, 2 * K_LANES), BF16) + _nbytes((seq // tk, V_ROWS, tk), BF16)
              + _nbytes((2 * HEAD_DIM, tq), BF16) + _nbytes((tq, V_DIM), BF16)
              + _nbytes((V_DIM, V7X_LANES), F32) + _nbytes((V7X_SUBLANES, V7X_LANES), F32)
              + _nbytes((tk // tq, tk, tq), F32))
    temps = 2 * _nbytes((V_ROWS, tq), F32) + 12 * _nbytes((tk, tq), F32)
    return pl.pallas_call(
        functools.partial(_attn_kernel, lam_init=lam_init),
        grid=grid, in_specs=in_specs, out_specs=out_specs,
        out_shape=jax.ShapeDtypeStruct((batch, ATTN_WIDTH, seq), BF16),
        scratch_shapes=[pltpu.VMEM((3, 2, K_LANES, tq), BF16),
                        pltpu.VMEM((2, tk, tq), F32),
                        pltpu.VMEM((2, tk, tq), F32),
                        pltpu.VMEM((2, V_ROWS, tq), F32)],
        compiler_params=pltpu.CompilerParams(
            dimension_semantics=("arbitrary", "arbitrary", "arbitrary"),
            vmem_limit_bytes=_vmem_limit(blocks, temps)),
        name="attn",
    )(cs, lq, g_col, dist, kn2, qt, kp.reshape(batch, seq, -1), vt)


def _mix_kernel(inner_ref, prev_ref, next_ref, bc_ref, o_ref, h_ref, g_in_ref, wg_ref,
                cw_ref, bg_ref, wco_ref, wao_ref, wo_ref, g_ref, out_ref, *, tiles_per_seq):
    t = pl.program_id(0) % tiles_per_seq
    h = h_ref[...]
    u = _rms(h, g_in_ref[...]).astype(BF16)
    gate = jax.nn.sigmoid(jnp.dot(u, wg_ref[...], preferred_element_type=F32) + bg_ref[...])
    x = inner_ref[...]
    tm = x.shape[0]
    prev_row = jnp.where(t == 0, 0.0, prev_ref[V7X_SUBLANES - 1:V7X_SUBLANES, :])
    next_row = jnp.where(t == tiles_per_seq - 1, 0.0, next_ref[0:1, :])
    row = lax.broadcasted_iota(jnp.int32, x.shape, 0)
    up = jnp.where(row == 0, prev_row, pltpu.roll(x, 1, 0))
    dn = jnp.where(row == tm - 1, next_row, pltpu.roll(x, tm - 1, 0))
    conv = up * cw_ref[0:1, :] + x * cw_ref[1:2, :] + dn * cw_ref[2:3, :]
    z = (bc_ref[...] * conv).astype(BF16)
    y_conv = jnp.dot(z, wco_ref[...], preferred_element_type=F32)
    y_attn = lax.dot_general(o_ref[0], wao_ref[...], (((0,), (0,)), ((), ())),
                             preferred_element_type=F32)
    merged = (gate[:, :D_MODEL] * y_conv + gate[:, D_MODEL:] * y_attn).astype(BF16)
    m = jnp.dot(merged, wo_ref[...], preferred_element_type=F32)
    out_ref[...] = h + _rms(m, g_ref[...])


def _mix(inner, bc, o, h, norm_g, wg, cw, bg, wco, wao, wo, layer, seq):
    n_tok = h.shape[0]
    tm = ROW_TILE
    hb = tm // V7X_SUBLANES
    n_hb = n_tok // V7X_SUBLANES
    grid = (n_tok // tm,)
    row = lambda i: (i, 0)
    in_specs = [
        pl.BlockSpec((tm, CONV_WIDTH), row),
        pl.BlockSpec((V7X_SUBLANES, CONV_WIDTH), lambda i: (jnp.maximum(i * hb - 1, 0), 0)),
        pl.BlockSpec((V7X_SUBLANES, CONV_WIDTH), lambda i: (jnp.minimum((i + 1) * hb, n_hb - 1), 0)),
        pl.BlockSpec((tm, CONV_WIDTH), row),
        pl.BlockSpec((1, ATTN_WIDTH, tm), lambda i: (i // (seq // tm), 0, i % (seq // tm))),
        pl.BlockSpec((tm, D_MODEL), row),
        _layer_spec((1, D_MODEL), 4 * layer),
        _layer_spec((D_MODEL, 2 * D_MODEL), layer),
        _layer_spec((CONV_K, CONV_WIDTH), layer),
        _layer_spec((1, 2 * D_MODEL), layer),
        _layer_spec((CONV_WIDTH, D_MODEL), layer),
        _layer_spec((ATTN_WIDTH, D_MODEL), layer),
        _layer_spec((D_MODEL, D_MODEL), layer),
        _layer_spec((1, D_MODEL), 4 * layer + 1),
    ]
    blocks = (2 * _nbytes((tm, CONV_WIDTH), F32) + _nbytes((tm, ATTN_WIDTH), BF16)
              + 2 * _nbytes((tm, D_MODEL), F32) + _nbytes((D_MODEL, 2 * D_MODEL), BF16)
              + _nbytes((2 * CONV_WIDTH + D_MODEL, D_MODEL), BF16) + 8 * _nbytes((8, 2 * D_MODEL), F32))
    temps = 10 * _nbytes((tm, D_MODEL), F32)
    return pl.pallas_call(
        functools.partial(_mix_kernel, tiles_per_seq=seq // tm),
        grid=grid, in_specs=in_specs, out_specs=pl.BlockSpec((tm, D_MODEL), row),
        out_shape=jax.ShapeDtypeStruct((n_tok, D_MODEL), F32),
        compiler_params=pltpu.CompilerParams(
            dimension_semantics=("arbitrary",), vmem_limit_bytes=_vmem_limit(blocks, temps)),
        name="mix",
    )(inner, inner, inner, bc, o, h, norm_g, wg, cw, bg, wco, wao, wo, norm_g)


FF_CHUNK = 1024


def _ffn_kernel(h_ref, g_in_ref, w1_ref, w2_ref, g_out_ref, out_ref):
    x = h_ref[...]
    u = _rms(x, g_in_ref[...]).astype(BF16)
    acc = jnp.zeros(x.shape, F32)
    for lo in range(0, D_FF, FF_CHUNK):
        hid = jnp.dot(u, w1_ref[:, lo:lo + FF_CHUNK], preferred_element_type=F32)
        hid = jnp.square(jnp.maximum(hid, 0.0)).astype(BF16)
        acc = acc + jnp.dot(hid, w2_ref[lo:lo + FF_CHUNK, :], preferred_element_type=F32)
    out_ref[...] = x + _rms(acc, g_out_ref[...])


def _ffn(h, norm_g, w1, w2, layer):
    n_tok = h.shape[0]
    tm = ROW_TILE
    row = lambda i: (i, 0)
    in_specs = [
        pl.BlockSpec((tm, D_MODEL), row),
        _layer_spec((1, D_MODEL), 4 * layer + 2),
        _layer_spec((D_MODEL, D_FF), layer),
        _layer_spec((D_FF, D_MODEL), layer),
        _layer_spec((1, D_MODEL), 4 * layer + 3),
    ]
    blocks = 2 * _nbytes((tm, D_MODEL), F32) + 2 * _nbytes((D_MODEL, D_FF), BF16)
    temps = 6 * _nbytes((tm, D_MODEL), F32)
    return pl.pallas_call(
        _ffn_kernel, grid=(n_tok // tm,), in_specs=in_specs,
        out_specs=pl.BlockSpec((tm, D_MODEL), row),
        out_shape=jax.ShapeDtypeStruct((n_tok, D_MODEL), F32),
        compiler_params=pltpu.CompilerParams(
            dimension_semantics=("arbitrary",), vmem_limit_bytes=_vmem_limit(blocks, temps)),
        name="ffn",
    )(h, norm_g, w1, w2, norm_g)


def _alibi_consts():
    slopes = jnp.exp2(-8.0 * jnp.arange(1, N_HEADS + 1, dtype=F32) / N_HEADS)
    cs = slopes * LOG2E
    x = cs[None, :] * jnp.arange(K_TILE, dtype=F32)[:, None]
    terms, rest = [], x
    for _ in range(N_AUX):
        t = rest.astype(BF16).astype(F32)
        terms.append(t)
        rest = rest - t
    aux = jnp.stack(terms, axis=-1)
    lanes = jnp.zeros((K_TILE, N_HEADS, 2, K_LANES), F32)
    lanes = lanes.at[:, :, :, HEAD_DIM:HEAD_DIM + N_AUX].set(aux[:, :, None, :])
    n_pos = K_TILE // Q_TILE
    qpos = jnp.arange(Q_TILE)[None, None, :] + Q_TILE * jnp.arange(n_pos)[:, None, None]
    dist = jnp.abs(qpos - jnp.arange(K_TILE)[None, :, None]).astype(F32)
    return jnp.stack([cs, 1.0 / cs]), lanes.reshape(K_TILE, 2 * N_HEADS * K_LANES), dist


def _prep_in_weights(w_in):
    cw, aw = CONV_WIDTH, ATTN_WIDTH
    depth = w_in.shape[0]
    w_conv = w_in[:, :, :3 * cw]
    w_q = w_in[:, :, 3 * cw:3 * cw + aw]
    w_k = w_in[:, :, 3 * cw + aw:3 * cw + 2 * aw]
    w_v = w_in[:, :, 3 * cw + 2 * aw:3 * cw + 3 * aw]
    w_g = w_in[:, :, 3 * cw + 3 * aw:]
    w_kp = jnp.pad(w_k.reshape(depth, D_MODEL, 2 * N_HEADS, HEAD_DIM),
                   ((0, 0), (0, 0), (0, 0), (0, K_LANES - HEAD_DIM))).reshape(depth, D_MODEL, -1)
    wnn = jnp.concatenate([w_conv, w_kp], axis=2).astype(BF16)
    wt = jnp.swapaxes(jnp.concatenate([w_q, w_v], axis=2), 1, 2).astype(BF16)
    return wnn, wt, w_g.astype(BF16)


def kernel(x, w_in, b_gate, conv_w, w_conv_out, lambda_qk, subln_g, w_attn_out, w_o, norm_g,
           w_ff1, w_ff2):
    batch, seq, _ = x.shape
    depth = w_in.shape[0]
    assert seq % ROW_TILE == 0 and seq % Q_TILE == 0 and K_TILE % Q_TILE == 0
    cs, kaux, dist = _alibi_consts()
    wnn, wt, wg = _prep_in_weights(w_in)
    wco, wao, wo = w_conv_out.astype(BF16), w_attn_out.astype(BF16), w_o.astype(BF16)
    w1, w2 = w_ff1.astype(BF16), w_ff2.astype(BF16)
    norm_g = norm_g.reshape(depth * 4, 1, D_MODEL)
    bg = b_gate.reshape(depth, 1, 2 * D_MODEL)
    g_col = subln_g.reshape(depth, V_DIM, 1)
    h = x.reshape(batch * seq, D_MODEL)
    for l in range(depth):
        lam_init = 0.8 - 0.6 * math.exp(-0.3 * l)
        inner, bc, kp, qt, vt, kn2 = _inproj(h, norm_g, wnn, wt, kaux, l, batch, seq)
        o = _attention(cs, lambda_qk, g_col, dist, kn2, qt, kp, vt, l, lam_init, batch, seq)
        h = _mix(inner, bc, o, h, norm_g, wg, conv_w, bg,
                 wco, wao, wo, l, seq)
        h = _ffn(h, norm_g, w1, w2, l)
    return h.reshape(batch, seq, D_MODEL)
```

```python
import functools
import math

import jax
import jax.numpy as jnp
from jax import lax
from jax.experimental import pallas as pl
from jax.experimental.pallas import tpu as pltpu

D_MODEL = 1024
CONV_WIDTH = 512
CONV_K = 3
N_HEADS = 4
HEAD_DIM = 64
V_DIM = 128
ATTN_WIDTH = N_HEADS * V_DIM
D_FF = 4 * D_MODEL
EPS = 1e-6

F32 = jnp.float32
BF16 = jnp.bfloat16

V7X_LANES = 128
V7X_SUBLANES = 8
V7X_BF16_ROWS_PER_VREG = 16
V7X_MXU_DIM = 256
V7X_VMEM_BYTES = 64 * 1024 * 1024

ROW_TILE = 512
Q_TILE = 2 * V7X_MXU_DIM
K_TILE = ROW_TILE
K_LANES = 2 * HEAD_DIM
N_AUX = 3
V_ROWS = V_DIM + V7X_BF16_ROWS_PER_VREG
LOG2E = math.log2(math.e)
Q_SCALE = HEAD_DIM ** -0.5 * LOG2E
M_INIT = -1e30
Q_BLOCKS_PER_STEP = 2
STEPS_PER_ITER = 6
SKIP_MARGIN = 152.0
NORM_SLACK = 1.01

W_NN_COLS = 3 * CONV_WIDTH + 2 * N_HEADS * K_LANES
OFF_BC, OFF_CC, OFF_K = 512, 1024, 1536


def _vmem_limit(block_bytes, temp_bytes):
    need = 2 * block_bytes + temp_bytes
    assert need < V7X_VMEM_BYTES, need
    return int(need)


def _nbytes(shape, dtype):
    return math.prod(shape) * jnp.dtype(dtype).itemsize


def _layer_spec(shape, index):
    return pl.BlockSpec((None,) + shape, lambda *_: (index,) + (0,) * len(shape))


def _rms(x, g):
    return x * lax.rsqrt(jnp.mean(x * x, axis=-1, keepdims=True) + EPS) * g


def _inproj_kernel(h_ref, g_ref, wnn_ref, wt_ref, kaux_ref,
                   inner_ref, bc_ref, kp_ref, qt_ref, vt_ref, kn2_ref):
    u = _rms(h_ref[...], g_ref[...]).astype(BF16)

    def nn(lo, hi):
        return jnp.dot(u, wnn_ref[:, lo:hi], preferred_element_type=F32)

    inner_ref[...] = nn(OFF_CC, OFF_K) * nn(0, OFF_BC)
    bc_ref[...] = nn(OFF_BC, OFF_CC)
    kf = nn(OFF_K, W_NN_COLS)
    kp_ref[...] = (kf + kaux_ref[...]).astype(BF16)
    for hm in range(2 * N_HEADS):
        sq = jnp.sum(jnp.square(kf[:, hm * K_LANES:(hm + 1) * K_LANES]), axis=1, keepdims=True)
        kn2_ref[0, hm:hm + 1, :] = jnp.broadcast_to(
            jnp.max(sq, axis=0, keepdims=True), (1, V7X_LANES))

    qv = lax.dot_general(wt_ref[...], u, (((1,), (1,)), ((), ())),
                         preferred_element_type=F32)
    qt_ref[0, 0] = (qv[:ATTN_WIDTH] * Q_SCALE).astype(BF16)
    tm = u.shape[0]
    ones_rows = jnp.where(
        lax.broadcasted_iota(jnp.int32, (V_ROWS - V_DIM, tm), 0) == 0, 1.0, 0.0).astype(BF16)
    for h in range(N_HEADS):
        lo = ATTN_WIDTH + h * V_DIM
        vt_ref[0, h, 0, :V_DIM, :] = qv[lo:lo + V_DIM].astype(BF16)
        vt_ref[0, h, 0, V_DIM:, :] = ones_rows


def _inproj(h, norm_g, wnn, wt, kaux, layer, batch, seq):
    n_tok = h.shape[0]
    tm = ROW_TILE
    tpb = seq // tm
    grid = (n_tok // tm,)
    row = lambda i: (i, 0)
    const = lambda i: (0, 0)
    out_shape = (
        jax.ShapeDtypeStruct((n_tok, CONV_WIDTH), F32),
        jax.ShapeDtypeStruct((n_tok, CONV_WIDTH), F32),
        jax.ShapeDtypeStruct((n_tok, 2 * N_HEADS * K_LANES), BF16),
        jax.ShapeDtypeStruct((batch, tpb, ATTN_WIDTH, tm), BF16),
        jax.ShapeDtypeStruct((batch, N_HEADS, tpb, V_ROWS, tm), BF16),
        jax.ShapeDtypeStruct((n_tok // tm, 2 * N_HEADS, V7X_LANES), F32),
    )
    in_specs = [
        pl.BlockSpec((tm, D_MODEL), row),
        _layer_spec((1, D_MODEL), 4 * layer),
        _layer_spec((D_MODEL, W_NN_COLS), layer),
        _layer_spec((2 * ATTN_WIDTH, D_MODEL), layer),
        pl.BlockSpec((tm, 2 * N_HEADS * K_LANES), const),
    ]
    out_specs = (
        pl.BlockSpec((tm, CONV_WIDTH), row),
        pl.BlockSpec((tm, CONV_WIDTH), row),
        pl.BlockSpec((tm, 2 * N_HEADS * K_LANES), row),
        pl.BlockSpec((1, 1, ATTN_WIDTH, tm), lambda i: (i // tpb, i % tpb, 0, 0)),
        pl.BlockSpec((1, N_HEADS, 1, V_ROWS, tm), lambda i: (i // tpb, 0, i % tpb, 0, 0)),
        pl.BlockSpec((1, 2 * N_HEADS, V7X_LANES), lambda i: (i, 0, 0)),
    )
    blocks = (_nbytes((tm, D_MODEL), F32) + _nbytes((D_MODEL, W_NN_COLS), BF16)
              + _nbytes((2 * ATTN_WIDTH, D_MODEL), BF16) + _nbytes((tm, 1024), F32)
              + 2 * _nbytes((tm, CONV_WIDTH), F32)
              + _nbytes((tm, 1024), BF16) + _nbytes((ATTN_WIDTH, tm), BF16)
              + _nbytes((N_HEADS, V_ROWS, tm), BF16))
    temps = 4 * _nbytes((tm, D_MODEL), F32) + _nbytes((2 * ATTN_WIDTH, tm), F32)
    return pl.pallas_call(
        _inproj_kernel, grid=grid, in_specs=in_specs, out_specs=out_specs, out_shape=out_shape,
        compiler_params=pltpu.CompilerParams(
            dimension_semantics=("arbitrary",), vmem_limit_bytes=_vmem_limit(blocks, temps)),
        name="inproj",
    )(h, norm_g, wnn, wt, kaux)


def _attn_kernel(*refs, lam_init):
    def body(j, carry):
        _attn_query_block(j, *refs, lam_init=lam_init)
        return carry

    lax.fori_loop(0, Q_BLOCKS_PER_STEP, body, 0)


def _attn_query_block(j, cs_ref, lq_ref, g_ref, dist_ref, kn2_ref, qt_ref, kp_ref, vt_ref, o_ref,
                      qv_ref, sa_ref, sb_ref, acc_ref, *, lam_init):
    h = pl.program_id(1)
    qi = pl.program_id(2) * Q_BLOCKS_PER_STEP + j
    tq, tk = Q_TILE, K_TILE
    n_kb = kp_ref.shape[1] // tk
    c = cs_ref[0, h]
    q0 = qi * tq
    kd = q0 // tk

    for v in range(3):
        for m in range(2):
            qv_ref[v, m, :HEAD_DIM, :] = qt_ref[0, j, m * HEAD_DIM:(m + 1) * HEAD_DIM, :]

    @pl.when((pl.program_id(0) == 0) & (h == 0) & (qi == 0))
    def _():
        aux_rows = lax.broadcasted_iota(jnp.int32, (K_LANES - HEAD_DIM, tq), 0) < N_AUX
        for v, sign in enumerate((1.0, -1.0, 0.0)):
            for m in range(2):
                qv_ref[v, m, HEAD_DIM:, :] = jnp.where(aux_rows, sign, 0.0).astype(BF16)

    ci = c * (q0 + lax.broadcasted_iota(jnp.int32, (1, tq), 1)).astype(F32)

    def scores(kb, variant, s_ref, bias):
        kblk = kp_ref[0, pl.ds(pl.multiple_of(kb * tk, tk), tk), :]
        cms = []
        for m in range(2):
            st = jnp.dot(kblk[:, m * K_LANES:(m + 1) * K_LANES], qv_ref[variant, m],
                         preferred_element_type=F32)
            if bias is not None:
                st = st + bias
            s_ref[m] = st
            cms.append(jnp.max(st, axis=0, keepdims=True))
        return tuple(cms)

    def fold(kb, s_ref, cms, off, ms, first=False):
        vblk = vt_ref[0, 0, kb]
        new_ms = []
        for m in range(2):
            m_new = jnp.maximum(ms[m], cms[m] + off)
            p = jnp.exp2(s_ref[m] - (m_new - off)).astype(BF16)
            pv = jnp.dot(vblk, p, preferred_element_type=F32)
            acc_ref[m] = pv if first else acc_ref[m] * jnp.exp2(ms[m] - m_new) + pv
            new_ms.append(m_new)
        return tuple(new_ms)

    def block_of(i, n_left):
        kb = jnp.where(i < n_left, kd - 1 - i, kd + 1 + i - n_left)
        left = kb < kd
        off = jnp.where(left, 1.0, -1.0) * (c * (kb * tk).astype(F32) - ci)
        return kb, jnp.where(left, 0, 1), off

    cms = scores(kd, 2, sa_ref, dist_ref[qi % (tk // tq)] * (-c))
    m0 = jnp.full((1, tq), M_INIT, F32)
    z0 = jnp.zeros((1, tq), F32)
    assert n_kb >= 2
    kb0, var0, off0 = block_of(jnp.int32(0), jnp.minimum(kd, 1))
    cms0 = scores(kb0, var0, sb_ref, None)
    ms = fold(kd, sa_ref, cms, z0, (m0, m0), first=True)

    need = None
    for m in range(2):
        qf = qt_ref[0, j, m * HEAD_DIM:(m + 1) * HEAD_DIM, :].astype(F32)
        qn = jnp.sqrt(jnp.max(jnp.sum(qf * qf, axis=0, keepdims=True), axis=1, keepdims=True))
        kn = jnp.sqrt(jnp.max(jnp.max(kn2_ref[0, 0, m], axis=0, keepdims=True), axis=1, keepdims=True))
        need_m = qn * kn * NORM_SLACK - jnp.min(cms[m], axis=1, keepdims=True)
        need = need_m if need is None else jnp.maximum(need, need_m)
    radius = (need + SKIP_MARGIN) * cs_ref[1, h] + 1.0
    q0f, kdf = q0.astype(F32), kd.astype(F32)
    lo = jnp.clip(jnp.floor((q0f - (tk - 1) - radius) / tk), 0.0, kdf).astype(jnp.int32)[0, 0]
    hi = jnp.clip(jnp.ceil((radius + q0f + (tq - 1)) / tk), kdf, n_kb - 1.0).astype(jnp.int32)[0, 0]
    lo = jnp.minimum(jnp.clip(lo, 0, kd), jnp.maximum(kd - 1, 0))
    hi = jnp.maximum(jnp.clip(hi, kd, n_kb - 1), jnp.minimum(kd + 1, n_kb - 1))
    n_left = kd - lo
    n_steps = hi - lo

    def step(i, s_next, s_prev, carry):
        ms, cms_prev, kb_prev, off_prev = carry
        kb, variant, off = block_of(i, n_left)
        cms_next = scores(kb, variant, s_next, None)
        ms = fold(kb_prev, s_prev, cms_prev, off_prev, ms)
        return ms, cms_next, kb, off

    bufs = (sa_ref, sb_ref)

    def steps(first, n, carry):
        for u in range(n):
            carry = step(first + u, bufs[u % 2], bufs[(u + 1) % 2], carry)
        return carry

    carry = (ms, cms0, kb0, off0)
    n_main = (n_steps - 1) // STEPS_PER_ITER
    carry = lax.fori_loop(
        0, n_main, lambda j, carry: steps(1 + j * STEPS_PER_ITER, STEPS_PER_ITER, carry), carry)
    done = 1 + n_main * STEPS_PER_ITER
    n_pairs = (n_steps - done) // 2
    carry = lax.fori_loop(0, n_pairs, lambda j, carry: steps(done + 2 * j, 2, carry), carry)
    done = done + 2 * n_pairs

    def tail_odd(carry):
        ms, cms, kb_last, off_last = steps(done, 1, carry)
        fold(kb_last, bufs[0], cms, off_last, ms)

    def tail_even(carry):
        ms, cms, kb_last, off_last = carry
        fold(kb_last, bufs[1], cms, off_last, ms)

    lax.cond(n_steps - done == 1, tail_odd, tail_even, carry)

    lq = lq_ref[...]
    lam = (jnp.exp(jnp.sum(lq[0:1] * lq[1:2], axis=-1, keepdims=True))
           - jnp.exp(jnp.sum(lq[2:3] * lq[3:4], axis=-1, keepdims=True)) + lam_init)
    o1 = acc_ref[0, :V_DIM, :] / acc_ref[0, V_DIM:V_DIM + 1, :]
    o2 = acc_ref[1, :V_DIM, :] / acc_ref[1, V_DIM:V_DIM + 1, :]
    ot = o1 - lam * o2
    y = ot * lax.rsqrt(jnp.mean(ot * ot, axis=0, keepdims=True) + EPS) * g_ref[...]
    y = y * (1.0 - lam_init)
    o_ref[0, j] = y.astype(BF16)


def _attention(cs, lq, g_col, dist, kn2, qt, kp, vt, layer, lam_init, batch, seq):
    tq, tk = Q_TILE, K_TILE
    qb = Q_BLOCKS_PER_STEP
    grid = (batch, N_HEADS, seq // (tq * qb))
    kn2 = kn2.reshape(batch, seq // tk, N_HEADS, 2, V7X_LANES).transpose(0, 2, 3, 1, 4)
    in_specs = [
        pl.BlockSpec(memory_space=pltpu.SMEM),
        _layer_spec((4, HEAD_DIM), layer),
        _layer_spec((V_DIM, 1), layer),
        pl.BlockSpec((tk // tq, tk, tq), lambda b, h, q: (0, 0, 0)),
        pl.BlockSpec((1, 1, 2, seq // tk, V7X_LANES), lambda b, h, q: (b, h, 0, 0, 0)),
        pl.BlockSpec((1, qb, 2 * HEAD_DIM, tq), lambda b, h, q: (b, q, h, 0)),
        pl.BlockSpec((1, seq, 2 * K_LANES), lambda b, h, q: (b, 0, h)),
        pl.BlockSpec((1, 1, seq // tk, V_ROWS, tk), lambda b, h, q: (b, h, 0, 0, 0)),
    ]
    out_specs = pl.BlockSpec((1, qb, V_DIM, tq), lambda b, h, q: (b, q, h, 0))
    blocks = (_nbytes((seq, 2 * K_LANES), BF16) + _nbytes((seq // tk, V_ROWS, tk), BF16)
              + qb * (_nbytes((2 * HEAD_DIM, tq), BF16) + _nbytes((V_DIM, tq), BF16))
              + _nbytes((V_DIM, V7X_LANES), F32) + _nbytes((V7X_SUBLANES, V7X_LANES), F32)
              + _nbytes((tk // tq, tk, tq), F32))
    temps = 2 * _nbytes((V_ROWS, tq), F32) + 12 * _nbytes((tk, tq), F32)
    return pl.pallas_call(
        functools.partial(_attn_kernel, lam_init=lam_init),
        grid=grid, in_specs=in_specs, out_specs=out_specs,
        out_shape=jax.ShapeDtypeStruct((batch, seq // tq, ATTN_WIDTH, tq), BF16),
        scratch_shapes=[pltpu.VMEM((3, 2, K_LANES, tq), BF16),
                        pltpu.VMEM((2, tk, tq), F32),
                        pltpu.VMEM((2, tk, tq), F32),
                        pltpu.VMEM((2, V_ROWS, tq), F32)],
        compiler_params=pltpu.CompilerParams(
            dimension_semantics=("arbitrary", "arbitrary", "arbitrary"),
            vmem_limit_bytes=_vmem_limit(blocks, temps)),
        name="attn",
    )(cs, lq, g_col, dist, kn2, qt, kp.reshape(batch, seq, -1), vt)


def _mix_kernel(inner_ref, prev_ref, next_ref, bc_ref, o_ref, h_ref, g_in_ref, wg_ref,
                cw_ref, bg_ref, wco_ref, wao_ref, wo_ref, g_ref, out_ref, *, tiles_per_seq):
    t = pl.program_id(0) % tiles_per_seq
    h = h_ref[...]
    u = _rms(h, g_in_ref[...]).astype(BF16)
    gate = jax.nn.sigmoid(jnp.dot(u, wg_ref[...], preferred_element_type=F32) + bg_ref[...])
    x = inner_ref[...]
    tm = x.shape[0]
    prev_row = jnp.where(t == 0, 0.0, prev_ref[V7X_SUBLANES - 1:V7X_SUBLANES, :])
    next_row = jnp.where(t == tiles_per_seq - 1, 0.0, next_ref[0:1, :])
    row = lax.broadcasted_iota(jnp.int32, x.shape, 0)
    up = jnp.where(row == 0, prev_row, pltpu.roll(x, 1, 0))
    dn = jnp.where(row == tm - 1, next_row, pltpu.roll(x, tm - 1, 0))
    conv = up * cw_ref[0:1, :] + x * cw_ref[1:2, :] + dn * cw_ref[2:3, :]
    z = (bc_ref[...] * conv).astype(BF16)
    y_conv = jnp.dot(z, wco_ref[...], preferred_element_type=F32)
    y_attn = lax.dot_general(o_ref[0, 0], wao_ref[...], (((0,), (0,)), ((), ())),
                             preferred_element_type=F32)
    merged = (gate[:, :D_MODEL] * y_conv + gate[:, D_MODEL:] * y_attn).astype(BF16)
    m = jnp.dot(merged, wo_ref[...], preferred_element_type=F32)
    out_ref[...] = h + _rms(m, g_ref[...])


def _mix(inner, bc, o, h, norm_g, wg, cw, bg, wco, wao, wo, layer, seq):
    n_tok = h.shape[0]
    tm = ROW_TILE
    hb = tm // V7X_SUBLANES
    n_hb = n_tok // V7X_SUBLANES
    grid = (n_tok // tm,)
    row = lambda i: (i, 0)
    in_specs = [
        pl.BlockSpec((tm, CONV_WIDTH), row),
        pl.BlockSpec((V7X_SUBLANES, CONV_WIDTH), lambda i: (jnp.maximum(i * hb - 1, 0), 0)),
        pl.BlockSpec((V7X_SUBLANES, CONV_WIDTH), lambda i: (jnp.minimum((i + 1) * hb, n_hb - 1), 0)),
        pl.BlockSpec((tm, CONV_WIDTH), row),
        pl.BlockSpec((1, 1, ATTN_WIDTH, tm), lambda i: (i // (seq // tm), i % (seq // tm), 0, 0)),
        pl.BlockSpec((tm, D_MODEL), row),
        _layer_spec((1, D_MODEL), 4 * layer),
        _layer_spec((D_MODEL, 2 * D_MODEL), layer),
        _layer_spec((CONV_K, CONV_WIDTH), layer),
        _layer_spec((1, 2 * D_MODEL), layer),
        _layer_spec((CONV_WIDTH, D_MODEL), layer),
        _layer_spec((ATTN_WIDTH, D_MODEL), layer),
        _layer_spec((D_MODEL, D_MODEL), layer),
        _layer_spec((1, D_MODEL), 4 * layer + 1),
    ]
    blocks = (2 * _nbytes((tm, CONV_WIDTH), F32) + _nbytes((tm, ATTN_WIDTH), BF16)
              + 2 * _nbytes((tm, D_MODEL), F32) + _nbytes((D_MODEL, 2 * D_MODEL), BF16)
              + _nbytes((2 * CONV_WIDTH + D_MODEL, D_MODEL), BF16) + 8 * _nbytes((8, 2 * D_MODEL), F32))
    temps = 10 * _nbytes((tm, D_MODEL), F32)
    return pl.pallas_call(
        functools.partial(_mix_kernel, tiles_per_seq=seq // tm),
        grid=grid, in_specs=in_specs, out_specs=pl.BlockSpec((tm, D_MODEL), row),
        out_shape=jax.ShapeDtypeStruct((n_tok, D_MODEL), F32),
        compiler_params=pltpu.CompilerParams(
            dimension_semantics=("arbitrary",), vmem_limit_bytes=_vmem_limit(blocks, temps)),
        name="mix",
    )(inner, inner, inner, bc, o, h, norm_g, wg, cw, bg, wco, wao, wo, norm_g)


FF_CHUNK = 1024


def _ffn_kernel(h_ref, g_in_ref, w1_ref, w2_ref, g_out_ref, out_ref):
    x = h_ref[...]
    u = _rms(x, g_in_ref[...]).astype(BF16)
    acc = jnp.zeros(x.shape, F32)
    for lo in range(0, D_FF, FF_CHUNK):
        hid = jnp.dot(u, w1_ref[:, lo:lo + FF_CHUNK], preferred_element_type=F32)
        hid = jnp.square(jnp.maximum(hid, 0.0)).astype(BF16)
        acc = acc + jnp.dot(hid, w2_ref[lo:lo + FF_CHUNK, :], preferred_element_type=F32)
    out_ref[...] = x + _rms(acc, g_out_ref[...])


def _ffn(h, norm_g, w1, w2, layer):
    n_tok = h.shape[0]
    tm = ROW_TILE
    row = lambda i: (i, 0)
    in_specs = [
        pl.BlockSpec((tm, D_MODEL), row),
        _layer_spec((1, D_MODEL), 4 * layer + 2),
        _layer_spec((D_MODEL, D_FF), layer),
        _layer_spec((D_FF, D_MODEL), layer),
        _layer_spec((1, D_MODEL), 4 * layer + 3),
    ]
    blocks = 2 * _nbytes((tm, D_MODEL), F32) + 2 * _nbytes((D_MODEL, D_FF), BF16)
    temps = 6 * _nbytes((tm, D_MODEL), F32)
    return pl.pallas_call(
        _ffn_kernel, grid=(n_tok // tm,), in_specs=in_specs,
        out_specs=pl.BlockSpec((tm, D_MODEL), row),
        out_shape=jax.ShapeDtypeStruct((n_tok, D_MODEL), F32),
        compiler_params=pltpu.CompilerParams(
            dimension_semantics=("arbitrary",), vmem_limit_bytes=_vmem_limit(blocks, temps)),
        name="ffn",
    )(h, norm_g, w1, w2, norm_g)


def _alibi_consts():
    slopes = jnp.exp2(-8.0 * jnp.arange(1, N_HEADS + 1, dtype=F32) / N_HEADS)
    cs = slopes * LOG2E
    x = cs[None, :] * jnp.arange(K_TILE, dtype=F32)[:, None]
    terms, rest = [], x
    for _ in range(N_AUX):
        t = rest.astype(BF16).astype(F32)
        terms.append(t)
        rest = rest - t
    aux = jnp.stack(terms, axis=-1)
    lanes = jnp.zeros((K_TILE, N_HEADS, 2, K_LANES), F32)
    lanes = lanes.at[:, :, :, HEAD_DIM:HEAD_DIM + N_AUX].set(aux[:, :, None, :])
    n_pos = K_TILE // Q_TILE
    qpos = jnp.arange(Q_TILE)[None, None, :] + Q_TILE * jnp.arange(n_pos)[:, None, None]
    dist = jnp.abs(qpos - jnp.arange(K_TILE)[None, :, None]).astype(F32)
    return jnp.stack([cs, 1.0 / cs]), lanes.reshape(K_TILE, 2 * N_HEADS * K_LANES), dist


def _prep_in_weights(w_in):
    cw, aw = CONV_WIDTH, ATTN_WIDTH
    depth = w_in.shape[0]
    w_conv = w_in[:, :, :3 * cw]
    w_q = w_in[:, :, 3 * cw:3 * cw + aw]
    w_k = w_in[:, :, 3 * cw + aw:3 * cw + 2 * aw]
    w_v = w_in[:, :, 3 * cw + 2 * aw:3 * cw + 3 * aw]
    w_g = w_in[:, :, 3 * cw + 3 * aw:]
    w_kp = jnp.pad(w_k.reshape(depth, D_MODEL, 2 * N_HEADS, HEAD_DIM),
                   ((0, 0), (0, 0), (0, 0), (0, K_LANES - HEAD_DIM))).reshape(depth, D_MODEL, -1)
    wnn = jnp.concatenate([w_conv, w_kp], axis=2).astype(BF16)
    wt = jnp.swapaxes(jnp.concatenate([w_q, w_v], axis=2), 1, 2).astype(BF16)
    return wnn, wt, w_g.astype(BF16)


def kernel(x, w_in, b_gate, conv_w, w_conv_out, lambda_qk, subln_g, w_attn_out, w_o, norm_g,
           w_ff1, w_ff2):
    batch, seq, _ = x.shape
    depth = w_in.shape[0]
    assert seq % ROW_TILE == 0 and K_TILE % Q_TILE == 0
    assert ROW_TILE == Q_TILE and seq % (Q_TILE * Q_BLOCKS_PER_STEP) == 0
    cs, kaux, dist = _alibi_consts()
    wnn, wt, wg = _prep_in_weights(w_in)
    wco, wao, wo = w_conv_out.astype(BF16), w_attn_out.astype(BF16), w_o.astype(BF16)
    w1, w2 = w_ff1.astype(BF16), w_ff2.astype(BF16)
    norm_g = norm_g.reshape(depth * 4, 1, D_MODEL)
    bg = b_gate.reshape(depth, 1, 2 * D_MODEL)
    g_col = subln_g.reshape(depth, V_DIM, 1)
    h = x.reshape(batch * seq, D_MODEL)
    for l in range(depth):
        lam_init = 0.8 - 0.6 * math.exp(-0.3 * l)
        inner, bc, kp, qt, vt, kn2 = _inproj(h, norm_g, wnn, wt, kaux, l, batch, seq)
        o = _attention(cs, lambda_qk, g_col, dist, kn2, qt, kp, vt, l, lam_init, batch, seq)
        h = _mix(inner, bc, o, h, norm_g, wg, conv_w, bg,
                 wco, wao, wo, l, seq)
        h = _ffn(h, norm_g, w1, w2, l)
    return h.reshape(batch, seq, D_MODEL)
```

```python
import functools
import math

import jax
import jax.numpy as jnp
from jax import lax
from jax.experimental import pallas as pl
from jax.experimental.pallas import tpu as pltpu

D_MODEL = 1024
CONV_WIDTH = 512
CONV_K = 3
N_HEADS = 4
HEAD_DIM = 64
V_DIM = 128
ATTN_WIDTH = N_HEADS * V_DIM
D_FF = 4 * D_MODEL
EPS = 1e-6

F32 = jnp.float32
BF16 = jnp.bfloat16

V7X_LANES = 128
V7X_SUBLANES = 8
V7X_BF16_ROWS_PER_VREG = 16
V7X_MXU_DIM = 256
V7X_VMEM_BYTES = 64 * 1024 * 1024

ROW_TILE = 512
Q_TILE = 2 * V7X_MXU_DIM
K_TILE = ROW_TILE
K_LANES = 2 * HEAD_DIM
N_AUX = 3
V_ROWS = V_DIM + V7X_BF16_ROWS_PER_VREG
LOG2E = math.log2(math.e)
Q_SCALE = HEAD_DIM ** -0.5 * LOG2E
M_INIT = -1e30
Q_BLOCKS_PER_STEP = 2
STEPS_PER_ITER = 6
SKIP_MARGIN = 152.0
NORM_SLACK = 1.01

W_NN_COLS = 3 * CONV_WIDTH + 2 * N_HEADS * K_LANES
OFF_BC, OFF_CC, OFF_K = 512, 1024, 1536


def _vmem_limit(block_bytes, temp_bytes):
    need = 2 * block_bytes + temp_bytes
    assert need < V7X_VMEM_BYTES, need
    return int(need)


def _nbytes(shape, dtype):
    return math.prod(shape) * jnp.dtype(dtype).itemsize


def _layer_spec(shape, index):
    return pl.BlockSpec((None,) + shape, lambda *_: (index,) + (0,) * len(shape))


def _rms(x, g):
    return x * lax.rsqrt(jnp.mean(x * x, axis=-1, keepdims=True) + EPS) * g


def _inproj_kernel(h_ref, g_ref, wnn_ref, wt_ref, kaux_ref,
                   inner_ref, bc_ref, kp_ref, qt_ref, vt_ref, kn2_ref):
    u = _rms(h_ref[...], g_ref[...]).astype(BF16)

    def nn(lo, hi):
        return jnp.dot(u, wnn_ref[:, lo:hi], preferred_element_type=F32)

    inner_ref[...] = nn(OFF_CC, OFF_K) * nn(0, OFF_BC)
    bc_ref[...] = nn(OFF_BC, OFF_CC)
    kf = nn(OFF_K, W_NN_COLS)
    kp_ref[...] = (kf + kaux_ref[...]).astype(BF16)
    for hm in range(2 * N_HEADS):
        sq = jnp.sum(jnp.square(kf[:, hm * K_LANES:(hm + 1) * K_LANES]), axis=1, keepdims=True)
        kn2_ref[0, hm:hm + 1, :] = jnp.broadcast_to(
            jnp.max(sq, axis=0, keepdims=True), (1, V7X_LANES))

    qv = lax.dot_general(wt_ref[...], u, (((1,), (1,)), ((), ())),
                         preferred_element_type=F32)
    qt_ref[0, 0] = (qv[:ATTN_WIDTH] * Q_SCALE).astype(BF16)
    tm = u.shape[0]
    ones_rows = jnp.where(
        lax.broadcasted_iota(jnp.int32, (V_ROWS - V_DIM, tm), 0) == 0, 1.0, 0.0).astype(BF16)
    for h in range(N_HEADS):
        lo = ATTN_WIDTH + h * V_DIM
        vt_ref[0, h, 0, :V_DIM, :] = qv[lo:lo + V_DIM].astype(BF16)
        vt_ref[0, h, 0, V_DIM:, :] = ones_rows


def _inproj(h, norm_g, wnn, wt, kaux, layer, batch, seq):
    n_tok = h.shape[0]
    tm = ROW_TILE
    tpb = seq // tm
    grid = (n_tok // tm,)
    row = lambda i: (i, 0)
    const = lambda i: (0, 0)
    out_shape = (
        jax.ShapeDtypeStruct((n_tok, CONV_WIDTH), F32),
        jax.ShapeDtypeStruct((n_tok, CONV_WIDTH), F32),
        jax.ShapeDtypeStruct((n_tok, 2 * N_HEADS * K_LANES), BF16),
        jax.ShapeDtypeStruct((batch, tpb, ATTN_WIDTH, tm), BF16),
        jax.ShapeDtypeStruct((batch, N_HEADS, tpb, V_ROWS, tm), BF16),
        jax.ShapeDtypeStruct((n_tok // tm, 2 * N_HEADS, V7X_LANES), F32),
    )
    in_specs = [
        pl.BlockSpec((tm, D_MODEL), row),
        _layer_spec((1, D_MODEL), 4 * layer),
        _layer_spec((D_MODEL, W_NN_COLS), layer),
        _layer_spec((2 * ATTN_WIDTH, D_MODEL), layer),
        pl.BlockSpec((tm, 2 * N_HEADS * K_LANES), const),
    ]
    out_specs = (
        pl.BlockSpec((tm, CONV_WIDTH), row),
        pl.BlockSpec((tm, CONV_WIDTH), row),
        pl.BlockSpec((tm, 2 * N_HEADS * K_LANES), row),
        pl.BlockSpec((1, 1, ATTN_WIDTH, tm), lambda i: (i // tpb, i % tpb, 0, 0)),
        pl.BlockSpec((1, N_HEADS, 1, V_ROWS, tm), lambda i: (i // tpb, 0, i % tpb, 0, 0)),
        pl.BlockSpec((1, 2 * N_HEADS, V7X_LANES), lambda i: (i, 0, 0)),
    )
    blocks = (_nbytes((tm, D_MODEL), F32) + _nbytes((D_MODEL, W_NN_COLS), BF16)
              + _nbytes((2 * ATTN_WIDTH, D_MODEL), BF16) + _nbytes((tm, 1024), F32)
              + 2 * _nbytes((tm, CONV_WIDTH), F32)
              + _nbytes((tm, 1024), BF16) + _nbytes((ATTN_WIDTH, tm), BF16)
              + _nbytes((N_HEADS, V_ROWS, tm), BF16))
    temps = 4 * _nbytes((tm, D_MODEL), F32) + _nbytes((2 * ATTN_WIDTH, tm), F32)
    return pl.pallas_call(
        _inproj_kernel, grid=grid, in_specs=in_specs, out_specs=out_specs, out_shape=out_shape,
        compiler_params=pltpu.CompilerParams(
            dimension_semantics=("arbitrary",), vmem_limit_bytes=_vmem_limit(blocks, temps)),
        name="inproj",
    )(h, norm_g, wnn, wt, kaux)


def _attn_kernel(*refs, lam_init):
    def body(j, carry):
        _attn_query_block(j, *refs, lam_init=lam_init)
        return carry

    lax.fori_loop(0, Q_BLOCKS_PER_STEP, body, 0)


def _attn_query_block(j, cs_ref, lq_ref, g_ref, dist_ref, kn2_ref, qt_ref, kp_ref, vt_ref, o_ref,
                      qv_ref, sa_ref, sb_ref, acc_ref, *, lam_init):
    h = pl.program_id(1)
    qi = pl.program_id(2) * Q_BLOCKS_PER_STEP + j
    tq, tk = Q_TILE, K_TILE
    n_kb = kp_ref.shape[1] // tk
    c = cs_ref[0, h]
    q0 = qi * tq
    kd = q0 // tk

    for v in range(3):
        for m in range(2):
            qv_ref[v, m, :HEAD_DIM, :] = qt_ref[0, j, m * HEAD_DIM:(m + 1) * HEAD_DIM, :]

    @pl.when((pl.program_id(0) == 0) & (h == 0) & (qi == 0))
    def _():
        aux_rows = lax.broadcasted_iota(jnp.int32, (K_LANES - HEAD_DIM, tq), 0) < N_AUX
        for v, sign in enumerate((1.0, -1.0, 0.0)):
            for m in range(2):
                qv_ref[v, m, HEAD_DIM:, :] = jnp.where(aux_rows, sign, 0.0).astype(BF16)

    ci = c * (q0 + lax.broadcasted_iota(jnp.int32, (1, tq), 1)).astype(F32)

    def scores(kb, variant, s_ref, bias):
        kblk = kp_ref[0, pl.ds(pl.multiple_of(kb * tk, tk), tk), :]
        cms = []
        for m in range(2):
            st = jnp.dot(kblk[:, m * K_LANES:(m + 1) * K_LANES], qv_ref[variant, m],
                         preferred_element_type=F32)
            if bias is not None:
                st = st + bias
            s_ref[m] = st
            cms.append(jnp.max(st, axis=0, keepdims=True))
        return tuple(cms)

    def fold(kb, s_ref, cms, off, ms, first=False):
        vblk = vt_ref[0, 0, kb]
        new_ms = []
        for m in range(2):
            m_new = jnp.maximum(ms[m], cms[m] + off)
            p = jnp.exp2(s_ref[m] - (m_new - off)).astype(BF16)
            pv = jnp.dot(vblk, p, preferred_element_type=F32)
            acc_ref[m] = pv if first else acc_ref[m] * jnp.exp2(ms[m] - m_new) + pv
            new_ms.append(m_new)
        return tuple(new_ms)

    def block_of(i, n_left):
        kb = jnp.where(i < n_left, kd - 1 - i, kd + 1 + i - n_left)
        left = kb < kd
        off = jnp.where(left, 1.0, -1.0) * (c * (kb * tk).astype(F32) - ci)
        return kb, jnp.where(left, 0, 1), off

    cms = scores(kd, 2, sa_ref, dist_ref[qi % (tk // tq)] * (-c))
    m0 = jnp.full((1, tq), M_INIT, F32)
    z0 = jnp.zeros((1, tq), F32)
    assert n_kb >= 2
    kb0, var0, off0 = block_of(jnp.int32(0), jnp.minimum(kd, 1))
    cms0 = scores(kb0, var0, sb_ref, None)
    ms = fold(kd, sa_ref, cms, z0, (m0, m0), first=True)

    need = None
    for m in range(2):
        qf = qt_ref[0, j, m * HEAD_DIM:(m + 1) * HEAD_DIM, :].astype(F32)
        qn = jnp.sqrt(jnp.max(jnp.sum(qf * qf, axis=0, keepdims=True), axis=1, keepdims=True))
        kn = jnp.sqrt(jnp.max(jnp.max(kn2_ref[0, 0, m], axis=0, keepdims=True), axis=1, keepdims=True))
        need_m = qn * kn * NORM_SLACK - jnp.min(cms[m], axis=1, keepdims=True)
        need = need_m if need is None else jnp.maximum(need, need_m)
    radius = (need + SKIP_MARGIN) * cs_ref[1, h] + 1.0
    q0f, kdf = q0.astype(F32), kd.astype(F32)
    lo = jnp.clip(jnp.floor((q0f - (tk - 1) - radius) / tk), 0.0, kdf).astype(jnp.int32)[0, 0]
    hi = jnp.clip(jnp.ceil((radius + q0f + (tq - 1)) / tk), kdf, n_kb - 1.0).astype(jnp.int32)[0, 0]
    lo = jnp.minimum(jnp.clip(lo, 0, kd), jnp.maximum(kd - 1, 0))
    hi = jnp.maximum(jnp.clip(hi, kd, n_kb - 1), jnp.minimum(kd + 1, n_kb - 1))
    n_left = kd - lo
    n_steps = hi - lo

    def step(i, s_next, s_prev, carry):
        ms, cms_prev, kb_prev, off_prev = carry
        kb, variant, off = block_of(i, n_left)
        cms_next = scores(kb, variant, s_next, None)
        ms = fold(kb_prev, s_prev, cms_prev, off_prev, ms)
        return ms, cms_next, kb, off

    bufs = (sa_ref, sb_ref)

    def steps(first, n, carry):
        for u in range(n):
            carry = step(first + u, bufs[u % 2], bufs[(u + 1) % 2], carry)
        return carry

    carry = (ms, cms0, kb0, off0)
    n_main = (n_steps - 1) // STEPS_PER_ITER
    carry = lax.fori_loop(
        0, n_main, lambda j, carry: steps(1 + j * STEPS_PER_ITER, STEPS_PER_ITER, carry), carry)
    done = 1 + n_main * STEPS_PER_ITER
    n_pairs = (n_steps - done) // 2
    carry = lax.fori_loop(0, n_pairs, lambda j, carry: steps(done + 2 * j, 2, carry), carry)
    done = done + 2 * n_pairs

    def tail_odd(carry):
        ms, cms, kb_last, off_last = steps(done, 1, carry)
        fold(kb_last, bufs[0], cms, off_last, ms)

    def tail_even(carry):
        ms, cms, kb_last, off_last = carry
        fold(kb_last, bufs[1], cms, off_last, ms)

    lax.cond(n_steps - done == 1, tail_odd, tail_even, carry)

    lq = lq_ref[...]
    lam = (jnp.exp(jnp.sum(lq[0:1] * lq[1:2], axis=-1, keepdims=True))
           - jnp.exp(jnp.sum(lq[2:3] * lq[3:4], axis=-1, keepdims=True)) + lam_init)
    o1 = acc_ref[0, :V_DIM, :] / acc_ref[0, V_DIM:V_DIM + 1, :]
    o2 = acc_ref[1, :V_DIM, :] / acc_ref[1, V_DIM:V_DIM + 1, :]
    ot = o1 - lam * o2
    y = ot * lax.rsqrt(jnp.mean(ot * ot, axis=0, keepdims=True) + EPS) * g_ref[...]
    y = y * (1.0 - lam_init)
    o_ref[0, j] = y.astype(BF16)


def _attention(cs, lq, g_col, dist, kn2, qt, kp, vt, layer, lam_init, batch, seq):
    tq, tk = Q_TILE, K_TILE
    qb = Q_BLOCKS_PER_STEP
    grid = (batch, N_HEADS, seq // (tq * qb))
    kn2 = kn2.reshape(batch, seq // tk, N_HEADS, 2, V7X_LANES).transpose(0, 2, 3, 1, 4)
    in_specs = [
        pl.BlockSpec(memory_space=pltpu.SMEM),
        _layer_spec((4, HEAD_DIM), layer),
        _layer_spec((V_DIM, 1), layer),
        pl.BlockSpec((tk // tq, tk, tq), lambda b, h, q: (0, 0, 0)),
        pl.BlockSpec((1, 1, 2, seq // tk, V7X_LANES), lambda b, h, q: (b, h, 0, 0, 0)),
        pl.BlockSpec((1, qb, 2 * HEAD_DIM, tq), lambda b, h, q: (b, q, h, 0)),
        pl.BlockSpec((1, seq, 2 * K_LANES), lambda b, h, q: (b, 0, h)),
        pl.BlockSpec((1, 1, seq // tk, V_ROWS, tk), lambda b, h, q: (b, h, 0, 0, 0)),
    ]
    out_specs = pl.BlockSpec((1, qb, V_DIM, tq), lambda b, h, q: (b, q, h, 0))
    blocks = (_nbytes((seq, 2 * K_LANES), BF16) + _nbytes((seq // tk, V_ROWS, tk), BF16)
              + qb * (_nbytes((2 * HEAD_DIM, tq), BF16) + _nbytes((V_DIM, tq), BF16))
              + _nbytes((V_DIM, V7X_LANES), F32) + _nbytes((V7X_SUBLANES, V7X_LANES), F32)
              + _nbytes((tk // tq, tk, tq), F32))
    temps = 2 * _nbytes((V_ROWS, tq), F32) + 12 * _nbytes((tk, tq), F32)
    return pl.pallas_call(
        functools.partial(_attn_kernel, lam_init=lam_init),
        grid=grid, in_specs=in_specs, out_specs=out_specs,
        out_shape=jax.ShapeDtypeStruct((batch, seq // tq, ATTN_WIDTH, tq), BF16),
        scratch_shapes=[pltpu.VMEM((3, 2, K_LANES, tq), BF16),
                        pltpu.VMEM((2, tk, tq), F32),
                        pltpu.VMEM((2, tk, tq), F32),
                        pltpu.VMEM((2, V_ROWS, tq), F32)],
        compiler_params=pltpu.CompilerParams(
            dimension_semantics=("arbitrary", "arbitrary", "arbitrary"),
            vmem_limit_bytes=_vmem_limit(blocks, temps)),
        name="attn",
    )(cs, lq, g_col, dist, kn2, qt, kp.reshape(batch, seq, -1), vt)


def _mix_kernel(inner_ref, prev_ref, next_ref, bc_ref, o_ref, h_ref, g_in_ref, wg_ref,
                cw_ref, bg_ref, wco_ref, wao_ref, wo_ref, g_ref, out_ref, *, tiles_per_seq):
    t = pl.program_id(0) % tiles_per_seq
    half = h_ref.shape[0] // 2
    prev_rows = (jnp.where(t == 0, 0.0, prev_ref[V7X_SUBLANES - 1:V7X_SUBLANES, :]),
                 inner_ref[half - 1:half, :])
    next_rows = (inner_ref[half:half + 1, :],
                 jnp.where(t == tiles_per_seq - 1, 0.0, next_ref[0:1, :]))
    for s in range(2):
        rows = pl.ds(s * half, half)
        h = h_ref[rows, :]
        u = _rms(h, g_in_ref[...]).astype(BF16)
        gate = jax.nn.sigmoid(jnp.dot(u, wg_ref[...], preferred_element_type=F32) + bg_ref[...])
        x = inner_ref[rows, :]
        row = lax.broadcasted_iota(jnp.int32, x.shape, 0)
        up = jnp.where(row == 0, prev_rows[s], pltpu.roll(x, 1, 0))
        dn = jnp.where(row == half - 1, next_rows[s], pltpu.roll(x, half - 1, 0))
        conv = up * cw_ref[0:1, :] + x * cw_ref[1:2, :] + dn * cw_ref[2:3, :]
        z = (bc_ref[rows, :] * conv).astype(BF16)
        y_conv = jnp.dot(z, wco_ref[...], preferred_element_type=F32)
        y_attn = lax.dot_general(o_ref[0, 0, :, s * half:(s + 1) * half], wao_ref[...],
                                 (((0,), (0,)), ((), ())), preferred_element_type=F32)
        merged = (gate[:, :D_MODEL] * y_conv + gate[:, D_MODEL:] * y_attn).astype(BF16)
        m = jnp.dot(merged, wo_ref[...], preferred_element_type=F32)
        out_ref[rows, :] = h + _rms(m, g_ref[...])


def _mix(inner, bc, o, h, norm_g, wg, cw, bg, wco, wao, wo, layer, seq):
    n_tok = h.shape[0]
    tm = ROW_TILE
    hb = tm // V7X_SUBLANES
    n_hb = n_tok // V7X_SUBLANES
    grid = (n_tok // tm,)
    row = lambda i: (i, 0)
    in_specs = [
        pl.BlockSpec((tm, CONV_WIDTH), row),
        pl.BlockSpec((V7X_SUBLANES, CONV_WIDTH), lambda i: (jnp.maximum(i * hb - 1, 0), 0)),
        pl.BlockSpec((V7X_SUBLANES, CONV_WIDTH), lambda i: (jnp.minimum((i + 1) * hb, n_hb - 1), 0)),
        pl.BlockSpec((tm, CONV_WIDTH), row),
        pl.BlockSpec((1, 1, ATTN_WIDTH, tm), lambda i: (i // (seq // tm), i % (seq // tm), 0, 0)),
        pl.BlockSpec((tm, D_MODEL), row),
        _layer_spec((1, D_MODEL), 4 * layer),
        _layer_spec((D_MODEL, 2 * D_MODEL), layer),
        _layer_spec((CONV_K, CONV_WIDTH), layer),
        _layer_spec((1, 2 * D_MODEL), layer),
        _layer_spec((CONV_WIDTH, D_MODEL), layer),
        _layer_spec((ATTN_WIDTH, D_MODEL), layer),
        _layer_spec((D_MODEL, D_MODEL), layer),
        _layer_spec((1, D_MODEL), 4 * layer + 1),
    ]
    blocks = (2 * _nbytes((tm, CONV_WIDTH), F32) + _nbytes((tm, ATTN_WIDTH), BF16)
              + 2 * _nbytes((tm, D_MODEL), F32) + _nbytes((D_MODEL, 2 * D_MODEL), BF16)
              + _nbytes((2 * CONV_WIDTH + D_MODEL, D_MODEL), BF16) + 8 * _nbytes((8, 2 * D_MODEL), F32))
    temps = 10 * _nbytes((tm, D_MODEL), F32)
    return pl.pallas_call(
        functools.partial(_mix_kernel, tiles_per_seq=seq // tm),
        grid=grid, in_specs=in_specs, out_specs=pl.BlockSpec((tm, D_MODEL), row),
        out_shape=jax.ShapeDtypeStruct((n_tok, D_MODEL), F32),
        compiler_params=pltpu.CompilerParams(
            dimension_semantics=("arbitrary",), vmem_limit_bytes=_vmem_limit(blocks, temps)),
        name="mix",
    )(inner, inner, inner, bc, o, h, norm_g, wg, cw, bg, wco, wao, wo, norm_g)


FF_CHUNK = 1024


def _ffn_kernel(h_ref, g_in_ref, w1_ref, w2_ref, g_out_ref, out_ref):
    half = h_ref.shape[0] // 2
    for rows in (pl.ds(0, half), pl.ds(half, half)):
        x = h_ref[rows, :]
        u = _rms(x, g_in_ref[...]).astype(BF16)
        acc = jnp.zeros(x.shape, F32)
        for lo in range(0, D_FF, FF_CHUNK):
            hid = jnp.dot(u, w1_ref[:, lo:lo + FF_CHUNK], preferred_element_type=F32)
            hid = jnp.square(jnp.maximum(hid, 0.0)).astype(BF16)
            acc = acc + jnp.dot(hid, w2_ref[lo:lo + FF_CHUNK, :], preferred_element_type=F32)
        out_ref[rows, :] = x + _rms(acc, g_out_ref[...])


def _ffn(h, norm_g, w1, w2, layer):
    n_tok = h.shape[0]
    tm = ROW_TILE
    row = lambda i: (i, 0)
    in_specs = [
        pl.BlockSpec((tm, D_MODEL), row),
        _layer_spec((1, D_MODEL), 4 * layer + 2),
        _layer_spec((D_MODEL, D_FF), layer),
        _layer_spec((D_FF, D_MODEL), layer),
        _layer_spec((1, D_MODEL), 4 * layer + 3),
    ]
    blocks = 2 * _nbytes((tm, D_MODEL), F32) + 2 * _nbytes((D_MODEL, D_FF), BF16)
    temps = 6 * _nbytes((tm, D_MODEL), F32)
    return pl.pallas_call(
        _ffn_kernel, grid=(n_tok // tm,), in_specs=in_specs,
        out_specs=pl.BlockSpec((tm, D_MODEL), row),
        out_shape=jax.ShapeDtypeStruct((n_tok, D_MODEL), F32),
        compiler_params=pltpu.CompilerParams(
            dimension_semantics=("arbitrary",), vmem_limit_bytes=_vmem_limit(blocks, temps)),
        name="ffn",
    )(h, norm_g, w1, w2, norm_g)


def _alibi_consts():
    slopes = jnp.exp2(-8.0 * jnp.arange(1, N_HEADS + 1, dtype=F32) / N_HEADS)
    cs = slopes * LOG2E
    x = cs[None, :] * jnp.arange(K_TILE, dtype=F32)[:, None]
    terms, rest = [], x
    for _ in range(N_AUX):
        t = rest.astype(BF16).astype(F32)
        terms.append(t)
        rest = rest - t
    aux = jnp.stack(terms, axis=-1)
    lanes = jnp.zeros((K_TILE, N_HEADS, 2, K_LANES), F32)
    lanes = lanes.at[:, :, :, HEAD_DIM:HEAD_DIM + N_AUX].set(aux[:, :, None, :])
    n_pos = K_TILE // Q_TILE
    qpos = jnp.arange(Q_TILE)[None, None, :] + Q_TILE * jnp.arange(n_pos)[:, None, None]
    dist = jnp.abs(qpos - jnp.arange(K_TILE)[None, :, None]).astype(F32)
    return jnp.stack([cs, 1.0 / cs]), lanes.reshape(K_TILE, 2 * N_HEADS * K_LANES), dist


def _prep_in_weights(w_in):
    cw, aw = CONV_WIDTH, ATTN_WIDTH
    depth = w_in.shape[0]
    w_conv = w_in[:, :, :3 * cw]
    w_q = w_in[:, :, 3 * cw:3 * cw + aw]
    w_k = w_in[:, :, 3 * cw + aw:3 * cw + 2 * aw]
    w_v = w_in[:, :, 3 * cw + 2 * aw:3 * cw + 3 * aw]
    w_g = w_in[:, :, 3 * cw + 3 * aw:]
    w_kp = jnp.pad(w_k.reshape(depth, D_MODEL, 2 * N_HEADS, HEAD_DIM),
                   ((0, 0), (0, 0), (0, 0), (0, K_LANES - HEAD_DIM))).reshape(depth, D_MODEL, -1)
    wnn = jnp.concatenate([w_conv, w_kp], axis=2).astype(BF16)
    wt = jnp.swapaxes(jnp.concatenate([w_q, w_v], axis=2), 1, 2).astype(BF16)
    return wnn, wt, w_g.astype(BF16)


def kernel(x, w_in, b_gate, conv_w, w_conv_out, lambda_qk, subln_g, w_attn_out, w_o, norm_g,
           w_ff1, w_ff2):
    batch, seq, _ = x.shape
    depth = w_in.shape[0]
    assert seq % ROW_TILE == 0 and K_TILE % Q_TILE == 0
    assert ROW_TILE == Q_TILE and seq % (Q_TILE * Q_BLOCKS_PER_STEP) == 0
    cs, kaux, dist = _alibi_consts()
    wnn, wt, wg = _prep_in_weights(w_in)
    wco, wao, wo = w_conv_out.astype(BF16), w_attn_out.astype(BF16), w_o.astype(BF16)
    w1, w2 = w_ff1.astype(BF16), w_ff2.astype(BF16)
    norm_g = norm_g.reshape(depth * 4, 1, D_MODEL)
    bg = b_gate.reshape(depth, 1, 2 * D_MODEL)
    g_col = subln_g.reshape(depth, V_DIM, 1)
    h = x.reshape(batch * seq, D_MODEL)
    for l in range(depth):
        lam_init = 0.8 - 0.6 * math.exp(-0.3 * l)
        inner, bc, kp, qt, vt, kn2 = _inproj(h, norm_g, wnn, wt, kaux, l, batch, seq)
        o = _attention(cs, lambda_qk, g_col, dist, kn2, qt, kp, vt, l, lam_init, batch, seq)
        h = _mix(inner, bc, o, h, norm_g, wg, conv_w, bg,
                 wco, wao, wo, l, seq)
        h = _ffn(h, norm_g, w1, w2, l)
    return h.reshape(batch, seq, D_MODEL)
```
